```python
import jax, jax.numpy as jnp
from jax import lax
import numpy as np

D_MODEL = 1024
BATCH = 2
SEQ = 16384
DEPTH = 4

BLOCK = 128
EPS = 1e-6
NEG = -1e30
GM_GROUPS = 4
GM_GROUP_DIM = 128
GM_WIDTH = GM_GROUPS * GM_GROUP_DIM
HEAD_DIM = 64
SW_HEADS = 8
SW_KV_HEADS = 2
SW_WIDTH = SW_HEADS * HEAD_DIM
SW_KV_WIDTH = SW_KV_HEADS * HEAD_DIM
WINDOW = 128
ROPE_THETA = 10000.0
SB_HEADS = 4
SB_HEAD_DIM = 128
SB_WIDTH = SB_HEADS * SB_HEAD_DIM
N_BRANCH = 3
IN_SPLITS = (
    GM_WIDTH, GM_WIDTH, GM_WIDTH,
    SW_WIDTH, SW_KV_WIDTH, SW_KV_WIDTH, SW_WIDTH,
    SB_WIDTH, SB_WIDTH, SB_WIDTH, SB_WIDTH,
    N_BRANCH * D_MODEL,
)
IN_WIDTH = sum(IN_SPLITS)

kernel_name = "hybrid_gmlp_swa_sink_stickbreak_block"


def rms_norm(x, g):
    xf = x.astype(jnp.float32)
    y = xf * lax.rsqrt(jnp.mean(xf * xf, axis=-1, keepdims=True) + EPS)
    return (y * g.astype(jnp.float32)).astype(x.dtype)


def rope(x, pos):
    half = HEAD_DIM // 2
    freqs = ROPE_THETA ** (-jnp.arange(half, dtype=jnp.float32) / half)
    ang = pos.astype(jnp.float32)[:, None] * freqs[None, :]
    cos = jnp.cos(ang)[None, :, None, :]
    sin = jnp.sin(ang)[None, :, None, :]
    xf = x.astype(jnp.float32)
    x1, x2 = xf[..., :half], xf[..., half:]
    out = jnp.concatenate([x1 * cos - x2 * sin, x2 * cos + x1 * sin], axis=-1)
    return out.astype(x.dtype)


def chunk_gmlp(u, v, w_s, b_s, g_v):
    B, S, _ = v.shape
    n = S // BLOCK
    vf = v.astype(jnp.float32)
    mu = jnp.mean(vf, axis=-1, keepdims=True)
    var = jnp.mean(jnp.square(vf - mu), axis=-1, keepdims=True)
    vn = ((vf - mu) * lax.rsqrt(var + EPS) * g_v.astype(jnp.float32)).astype(v.dtype)
    vn = vn.reshape(B, n, BLOCK, GM_GROUPS, GM_GROUP_DIM)
    causal = jnp.tril(jnp.ones((BLOCK, BLOCK), dtype=bool))
    w = jnp.where(causal[None], w_s, jnp.zeros_like(w_s))
    mixed = jnp.einsum('gts,bnsgc->bntgc', w, vn)
    mixed = mixed + b_s.T[None, None, :, :, None]
    return u * mixed.reshape(B, S, GM_WIDTH)


def sliding_window_attention(q, k, v, sinks):
    B, S, H, Dh = q.shape
    n = S // BLOCK
    rep = H // SW_KV_HEADS
    qb = q.reshape(B, n, BLOCK, SW_KV_HEADS, rep, Dh)

    def band(t):
        tb = t.reshape(B, n, BLOCK, SW_KV_HEADS, Dh)
        prev = jnp.pad(tb[:, :-1], ((0, 0), (1, 0), (0, 0), (0, 0), (0, 0)))
        return jnp.concatenate([prev, tb], axis=2)

    kb, vb = band(k), band(v)
    s = jnp.einsum('bnqgrd,bnkgd->bngrqk', qb, kb).astype(jnp.float32) * (Dh ** -0.5)
    qi = jnp.arange(BLOCK)[:, None]
    kj = jnp.arange(2 * BLOCK)[None, :]
    diff = qi + BLOCK - kj
    local = (diff >= 0) & (diff < WINDOW)
    blk = jnp.arange(n)[:, None, None]
    valid = local[None] & ((blk > 0) | (kj >= BLOCK)[None])
    s = jnp.where(valid[None, :, None, None], s, NEG)
    sink = jnp.broadcast_to(
        sinks.astype(jnp.float32).reshape(1, 1, SW_KV_HEADS, rep, 1, 1), s.shape[:-1] + (1,))
    p = jax.nn.softmax(jnp.concatenate([s, sink], axis=-1), axis=-1)[..., :-1]
    o = jnp.einsum('bngrqk,bnkgd->bnqgrd', p.astype(v.dtype), vb)
    return o.reshape(B, S, H * Dh)


def stick_breaking_attention(q, k, v):
    B, S, H, Dh = q.shape
    n = S // BLOCK
    qt = q.transpose(0, 2, 1, 3) * (Dh ** -0.5)
    kt = k.transpose(0, 2, 1, 3)
    vt = v.transpose(0, 2, 1, 3)
    qoff = jnp.arange(BLOCK)
    outs = []
    for i in range(n):
        kl = (i + 1) * BLOCK
        z = jnp.einsum('bhqd,bhkd->bhqk', qt[:, :, i * BLOCK:kl],
                       kt[:, :, :kl]).astype(jnp.float32)
        before = (jnp.arange(kl)[None, :] < (i * BLOCK + qoff)[:, None])[None, None]
        log_fail = jnp.where(before, jax.nn.log_sigmoid(-z), 0.0)
        a = jnp.where(before, jnp.exp(z + lax.cumsum(log_fail, axis=3, reverse=True)), 0.0)
        outs.append(jnp.einsum('bhqk,bhkd->bhqd', a.astype(v.dtype), vt[:, :, :kl]))
    o = jnp.concatenate(outs, axis=2)
    return o.transpose(0, 2, 1, 3).reshape(B, S, H * Dh)


def hybrid_layer(x, pos, w_in, gm_w_s, gm_b_s, gm_norm_gain, sw_sinks,
                 w_branch_a, w_branch_b, w_branch_c, b_merge, w_out, g_pre, g_post):
    B, S, _ = x.shape
    h = rms_norm(x, g_pre)
    p = h @ w_in
    offsets = np.cumsum(np.array(IN_SPLITS))[:-1].tolist()
    (u_a, v_a, gate_a, q_b, k_b, v_b, gate_b,
     q_c, k_c, v_c, gate_c, merge_logits) = jnp.split(p, offsets, axis=-1)

    y_a = chunk_gmlp(u_a, v_a, gm_w_s, gm_b_s, gm_norm_gain) * jax.nn.silu(gate_a)

    qh = rope(q_b.reshape(B, S, SW_HEADS, HEAD_DIM), pos)
    kh = rope(k_b.reshape(B, S, SW_KV_HEADS, HEAD_DIM), pos)
    vh = v_b.reshape(B, S, SW_KV_HEADS, HEAD_DIM)
    y_b = sliding_window_attention(qh, kh, vh, sw_sinks) * jax.nn.silu(gate_b)

    y_c = stick_breaking_attention(q_c.reshape(B, S, SB_HEADS, SB_HEAD_DIM),
                                   k_c.reshape(B, S, SB_HEADS, SB_HEAD_DIM),
                                   v_c.reshape(B, S, SB_HEADS, SB_HEAD_DIM)) * jax.nn.silu(gate_c)

    gates = jax.nn.sigmoid(merge_logits.reshape(B, S, N_BRANCH, D_MODEL) + b_merge)
    merged = (gates[:, :, 0] * (y_a @ w_branch_a)
              + gates[:, :, 1] * (y_b @ w_branch_b)
              + gates[:, :, 2] * (y_c @ w_branch_c))
    out = merged @ w_out
    return x + rms_norm(out, g_post)


def setup_inputs(seed: int = 0) -> dict:
    key = jax.random.key(seed)
    ks = jax.random.split(key, 16)
    f32 = jnp.float32
    nrm = lambda k, shape, s: jax.random.normal(k, shape, f32) * s
    return {
        "x": nrm(ks[0], (BATCH, SEQ, D_MODEL), 1.0),
        "w_in": nrm(ks[1], (DEPTH, D_MODEL, IN_WIDTH), D_MODEL ** -0.5),
        "gm_w_s": nrm(ks[2], (DEPTH, GM_GROUPS, BLOCK, BLOCK), BLOCK ** -0.5),
        "gm_b_s": 1.0 + nrm(ks[3], (DEPTH, GM_GROUPS, BLOCK), 0.02),
        "gm_norm_gain": 1.0 + nrm(ks[4], (DEPTH, GM_WIDTH), 0.02),
        "sw_sinks": nrm(ks[5], (DEPTH, SW_HEADS), 1.0),
        "w_branch_a": nrm(ks[6], (DEPTH, GM_WIDTH, D_MODEL), GM_WIDTH ** -0.5),
        "w_branch_b": nrm(ks[7], (DEPTH, SW_WIDTH, D_MODEL), SW_WIDTH ** -0.5),
        "w_branch_c": nrm(ks[8], (DEPTH, SB_WIDTH, D_MODEL), SB_WIDTH ** -0.5),
        "b_merge": nrm(ks[9], (DEPTH, N_BRANCH, D_MODEL), 0.02),
        "w_out": nrm(ks[10], (DEPTH, D_MODEL, D_MODEL), D_MODEL ** -0.5),
        "g_pre": 1.0 + nrm(ks[11], (DEPTH, D_MODEL), 0.02),
        "g_post": 1.0 + nrm(ks[12], (DEPTH, D_MODEL), 0.02),
    }


def reference(x, w_in, gm_w_s, gm_b_s, gm_norm_gain, sw_sinks, w_branch_a, w_branch_b,
              w_branch_c, b_merge, w_out, g_pre, g_post):
    pos = jnp.arange(x.shape[1])
    for l in range(DEPTH):
        x = hybrid_layer(x, pos, w_in[l], gm_w_s[l], gm_b_s[l], gm_norm_gain[l], sw_sinks[l],
                         w_branch_a[l], w_branch_b[l], w_branch_c[l], b_merge[l], w_out[l],
                         g_pre[l], g_post[l])
    return x
```

```python
import functools

import jax
import jax.numpy as jnp
from jax import lax
from jax.experimental import pallas as pl
from jax.experimental.pallas import tpu as pltpu

F32 = jnp.float32
BF16 = jnp.bfloat16

D_MODEL = 1024
BLOCK = 128
EPS = 1e-6
NEG = -1e30
GM_GROUPS = 4
GM_WIDTH = 512
HEAD_DIM = 64
SW_HEADS = 8
SW_KV_HEADS = 2
SW_WIDTH = 512
SW_KV_WIDTH = 128
ROPE_THETA = 10000.0
SB_HEADS = 4
SB_HEAD_DIM = 128
SB_WIDTH = 512
N_BRANCH = 3
IN_WIDTH = 7936

V7X_VMEM_BYTES = 64 * 1024 * 1024
LANES = 128
VMEM_LIMIT = V7X_VMEM_BYTES - 8 * 1024 * 1024

PA_WIDTH = 3 * GM_WIDTH
PB_WIDTH = 2 * SW_WIDTH + 2 * SW_KV_WIDTH
PC_WIDTH = 4 * SB_WIDTH
PG_WIDTH = N_BRANCH * D_MODEL
PROJ_CHUNK = 256

SB_EXP_ZERO = 104.0
SB_STOP = SB_EXP_ZERO * 1.1


def _proj_chunks():
    segs = [
        (0, 1536, 0, 0, 1.0),
        (1536, 2048, 1, 0, HEAD_DIM ** -0.5),
        (2048, 2304, 1, 2 * SW_WIDTH, 1.0),
        (2304, 2816, 1, SW_WIDTH, 1.0),
        (2816, 3328, 2, 0, SB_HEAD_DIM ** -0.5),
        (3328, 4864, 2, SB_WIDTH, 1.0),
        (4864, 7936, 3, 0, 1.0),
    ]
    out = []
    for lo, hi, idx, dst, scale in segs:
        for c in range(lo, hi, PROJ_CHUNK):
            out.append((c, idx, dst + c - lo, scale))
    return out


def _sigmoid(x):
    return 1.0 / (1.0 + jnp.exp(-x))


def _proj_kernel(x_ref, g_ref, w_ref, pa_ref, pb_ref, pc_ref, pg_ref):
    x = x_ref[...]
    ms = jnp.mean(x * x, axis=-1, keepdims=True)
    h = (x * lax.rsqrt(ms + EPS) * g_ref[...]).astype(BF16)
    outs = (pa_ref, pb_ref, pc_ref, pg_ref)
    for src, idx, dst, scale in _proj_chunks():
        r = jnp.dot(h, w_ref[:, src:src + PROJ_CHUNK], preferred_element_type=F32)
        if scale != 1.0:
            r = r * scale
        outs[idx][:, dst:dst + PROJ_CHUNK] = r.astype(BF16)


def _proj(x, g_pre, w_in, tm):
    m = x.shape[0]
    return pl.pallas_call(
        _proj_kernel,
        grid=(m // tm,),
        in_specs=[
            pl.BlockSpec((tm, D_MODEL), lambda i: (i, 0)),
            pl.BlockSpec((1, D_MODEL), lambda i: (0, 0)),
            pl.BlockSpec((D_MODEL, IN_WIDTH), lambda i: (0, 0), pipeline_mode=pl.Buffered(1)),
        ],
        out_specs=[
            pl.BlockSpec((tm, PA_WIDTH), lambda i: (i, 0)),
            pl.BlockSpec((tm, PB_WIDTH), lambda i: (i, 0)),
            pl.BlockSpec((tm, PC_WIDTH), lambda i: (i, 0)),
            pl.BlockSpec((tm, PG_WIDTH), lambda i: (i, 0)),
        ],
        out_shape=[
            jax.ShapeDtypeStruct((m, PA_WIDTH), BF16),
            jax.ShapeDtypeStruct((m, PB_WIDTH), BF16),
            jax.ShapeDtypeStruct((m, PC_WIDTH), BF16),
            jax.ShapeDtypeStruct((m, PG_WIDTH), BF16),
        ],
        compiler_params=pltpu.CompilerParams(
            dimension_semantics=("arbitrary",), vmem_limit_bytes=VMEM_LIMIT),
        name="proj",
    )(x, g_pre, w_in)


def _gmlp_kernel(u_ref, v_ref, gate_ref, ws_ref, bs_ref, gv_ref, o_ref, *, tm):
    row = lax.broadcasted_iota(jnp.int32, (BLOCK, BLOCK), 0)
    col = lax.broadcasted_iota(jnp.int32, (BLOCK, BLOCK), 1)
    causal = col <= row
    w = [jnp.where(causal, ws_ref[g], 0.0).astype(BF16) for g in range(GM_GROUPS)]
    for c in range(tm // BLOCK):
        rows = slice(c * BLOCK, (c + 1) * BLOCK)
        v = v_ref[rows, :].astype(F32)
        mu = jnp.mean(v, axis=-1, keepdims=True)
        d = v - mu
        var = jnp.mean(d * d, axis=-1, keepdims=True)
        vn = (d * lax.rsqrt(var + EPS) * gv_ref[...]).astype(BF16)
        for g in range(GM_GROUPS):
            cols = slice(g * BLOCK, (g + 1) * BLOCK)
            mixed = jnp.dot(w[g], vn[:, cols], preferred_element_type=F32) + bs_ref[:, g:g + 1]
            gate = gate_ref[rows, cols].astype(F32)
            y = u_ref[rows, cols].astype(F32) * mixed * (gate * _sigmoid(gate))
            o_ref[rows, cols] = y.astype(BF16)


def _gmlp(pa, w_s, b_s_t, g_v, tm):
    m = pa.shape[0]
    return pl.pallas_call(
        functools.partial(_gmlp_kernel, tm=tm),
        grid=(m // tm,),
        in_specs=[
            pl.BlockSpec((tm, GM_WIDTH), lambda i: (i, 0)),
            pl.BlockSpec((tm, GM_WIDTH), lambda i: (i, 1)),
            pl.BlockSpec((tm, GM_WIDTH), lambda i: (i, 2)),
            pl.BlockSpec((GM_GROUPS, BLOCK, BLOCK), lambda i: (0, 0, 0)),
            pl.BlockSpec((BLOCK, GM_GROUPS), lambda i: (0, 0)),
            pl.BlockSpec((1, GM_WIDTH), lambda i: (0, 0)),
        ],
        out_specs=pl.BlockSpec((tm, GM_WIDTH), lambda i: (i, 0)),
        out_shape=jax.ShapeDtypeStruct((m, GM_WIDTH), BF16),
        compiler_params=pltpu.CompilerParams(
            dimension_semantics=("arbitrary",), vmem_limit_bytes=VMEM_LIMIT),
        name="gmlp",
    )(pa, pa, pa, w_s, b_s_t, g_v)


def _swa_kernel(sinks_ref, q_ref, gate_ref, k_ref, v_ref, cos_ref, sin_ref, o_ref,
                k2_ref, v2_ref, *, tq):
    j = pl.program_id(1)

    @pl.when(j == 0)
    def _():
        k2_ref[...] = jnp.zeros(k2_ref.shape, BF16)
        v2_ref[...] = jnp.zeros(v2_ref.shape, BF16)

    lane = lax.broadcasted_iota(jnp.int32, (BLOCK, LANES), 1)
    rot_fwd = (lane % HEAD_DIM) < (HEAD_DIM // 2)
    lo = lane < HEAD_DIM
    zero = jnp.zeros((BLOCK, LANES), F32)

    def rope(x, cos, sin):
        rot = jnp.where(rot_fwd, pltpu.roll(x, LANES - HEAD_DIM // 2, 1),
                        pltpu.roll(x, HEAD_DIM // 2, 1))
        return x * cos + rot * sin

    qi = lax.broadcasted_iota(jnp.int32, (BLOCK, 4 * BLOCK), 0)
    kj = lax.broadcasted_iota(jnp.int32, (BLOCK, 4 * BLOCK), 1) % (2 * BLOCK)
    local = jnp.logical_and(kj > qi, kj <= qi + BLOCK)
    first_head = lax.broadcasted_iota(jnp.int32, (BLOCK, 4 * BLOCK), 1) < 2 * BLOCK

    def place(dst_ref, x, xr):
        dst_ref[0, BLOCK:2 * BLOCK, :] = jnp.where(lo, x, zero).astype(BF16)
        dst_ref[0, 3 * BLOCK:4 * BLOCK, :] = jnp.where(lo, zero, xr).astype(BF16)
        dst_ref[1, BLOCK:2 * BLOCK, :] = jnp.where(lo, xr, zero).astype(BF16)
        dst_ref[1, 3 * BLOCK:4 * BLOCK, :] = jnp.where(lo, zero, x).astype(BF16)

    def shift(dst_ref):
        for g in range(SW_KV_HEADS):
            dst_ref[g, 0:BLOCK, :] = dst_ref[g, BLOCK:2 * BLOCK, :]
            dst_ref[g, 2 * BLOCK:3 * BLOCK, :] = dst_ref[g, 3 * BLOCK:4 * BLOCK, :]

    for b in range(tq // BLOCK):
        rows = slice(b * BLOCK, (b + 1) * BLOCK)
        cos = cos_ref[rows, :]
        sin = sin_ref[rows, :]
        k = rope(k_ref[rows, :].astype(F32), cos, sin)
        v = v_ref[rows, :].astype(F32)
        place(k2_ref, k, pltpu.roll(k, HEAD_DIM, 1))
        place(v2_ref, v, pltpu.roll(v, HEAD_DIM, 1))
        if b == 0:
            valid = jnp.logical_and(local, kj >= jnp.where(j == 0, BLOCK, 0))
        else:
            valid = local
        for pr in range(SW_HEADS // 2):
            g = pr // 2
            cols = slice(pr * LANES, (pr + 1) * LANES)
            qp = rope(q_ref[rows, cols].astype(F32), cos, sin).astype(BF16)
            s = lax.dot_general(qp, k2_ref[g], (((1,), (1,)), ((), ())),
                                preferred_element_type=F32)
            s = jnp.where(valid, s, NEG)
            sink0 = sinks_ref[2 * pr]
            sink1 = sinks_ref[2 * pr + 1]
            m0 = jnp.maximum(jnp.max(s[:, :2 * BLOCK], axis=-1, keepdims=True), sink0)
            m1 = jnp.maximum(jnp.max(s[:, 2 * BLOCK:], axis=-1, keepdims=True), sink1)
            p = jnp.exp(s - jnp.where(first_head, m0, m1))
            l0 = jnp.sum(p[:, :2 * BLOCK], axis=-1, keepdims=True) + jnp.exp(sink0 - m0)
            l1 = jnp.sum(p[:, 2 * BLOCK:], axis=-1, keepdims=True) + jnp.exp(sink1 - m1)
            o = jnp.dot(p.astype(BF16), v2_ref[g], preferred_element_type=F32)
            o = o * jnp.where(lo, 1.0 / l0, 1.0 / l1)
            gate = gate_ref[rows, cols].astype(F32)
            o_ref[rows, cols] = (o * (gate * _sigmoid(gate))).astype(BF16)
        shift(k2_ref)
        shift(v2_ref)


def _swa(pb, sinks, cos_t, sin_t, batch, seq, tq):
    m = pb.shape[0]
    nj = seq // tq
    kcol = 2 * SW_WIDTH // SW_KV_WIDTH
    return pl.pallas_call(
        functools.partial(_swa_kernel, tq=tq),
        grid=(batch, nj),
        in_specs=[
            pl.BlockSpec(memory_space=pltpu.SMEM),
            pl.BlockSpec((tq, SW_WIDTH), lambda b, j: (b * nj + j, 0)),
            pl.BlockSpec((tq, SW_WIDTH), lambda b, j: (b * nj + j, 1)),
            pl.BlockSpec((tq, SW_KV_WIDTH), lambda b, j: (b * nj + j, kcol)),
            pl.BlockSpec((tq, SW_KV_WIDTH), lambda b, j: (b * nj + j, kcol + 1)),
            pl.BlockSpec((tq, LANES), lambda b, j: (j, 0)),
            pl.BlockSpec((tq, LANES), lambda b, j: (j, 0)),
        ],
        out_specs=pl.BlockSpec((tq, SW_WIDTH), lambda b, j: (b * nj + j, 0)),
        out_shape=jax.ShapeDtypeStruct((m, SW_WIDTH), BF16),
        scratch_shapes=[
            pltpu.VMEM((SW_KV_HEADS, 4 * BLOCK, LANES), BF16),
            pltpu.VMEM((SW_KV_HEADS, 4 * BLOCK, LANES), BF16),
        ],
        compiler_params=pltpu.CompilerParams(
            dimension_semantics=("arbitrary", "arbitrary"), vmem_limit_bytes=VMEM_LIMIT),
        name="swa",
    )(sinks, pb, pb, pb, pb, cos_t, sin_t)


def _sb_kernel(q_ref, k_ref, v_ref, gate_ref, o_ref, *, tq):
    c = pl.program_id(2)
    row = lax.broadcasted_iota(jnp.int32, (BLOCK, BLOCK), 0)
    col = lax.broadcasted_iota(jnp.int32, (BLOCK, BLOCK), 1)
    before = col < row
    suffix = jnp.where(row >= col, 1.0, 0.0).astype(BF16)
    nqb = tq // BLOCK

    def tile(q, m, carry, acc, diag):
        k0 = pl.multiple_of(m * BLOCK, BLOCK)
        kt = k_ref[pl.ds(k0, BLOCK), :]
        vt = v_ref[pl.ds(k0, BLOCK), :]
        z = lax.dot_general(q, kt, (((1,), (1,)), ((), ())), preferred_element_type=F32)
        fail = jnp.maximum(z, 0.0) + jnp.log1p(jnp.exp(-jnp.abs(z)))
        if diag:
            fail = jnp.where(before, fail, 0.0)
        hi = fail.astype(BF16)
        lo = (fail - hi.astype(F32)).astype(BF16)
        cs = (jnp.dot(hi, suffix, preferred_element_type=F32)
              + jnp.dot(lo, suffix, preferred_element_type=F32))
        a = jnp.exp(z - cs - carry)
        if diag:
            a = jnp.where(before, a, 0.0)
        acc = acc + jnp.dot(a.astype(BF16), vt, preferred_element_type=F32)
        return carry + cs[:, 0:1], acc

    def qblock(il, _):
        i = c * nqb + il
        r0 = pl.multiple_of(il * BLOCK, BLOCK)
        q = q_ref[pl.ds(r0, BLOCK), :]
        carry, acc = tile(q, i, jnp.zeros((BLOCK, 1), F32),
                          jnp.zeros((BLOCK, SB_HEAD_DIM), F32), True)

        def cond(st):
            m, least, _, _ = st
            return jnp.logical_and(m >= 0, least < SB_STOP)

        def body(st):
            m, _, carry, acc = st
            carry, acc = tile(q, m, carry, acc, False)
            return m - 1, jnp.min(carry), carry, acc

        _, _, _, acc = lax.while_loop(cond, body, (i - 1, jnp.min(carry), carry, acc))
        gate = gate_ref[pl.ds(r0, BLOCK), :].astype(F32)
        o_ref[pl.ds(r0, BLOCK), :] = (acc * (gate * _sigmoid(gate))).astype(BF16)
        return 0

    lax.fori_loop(0, nqb, qblock, 0)


def _sb(pc, batch, seq, tq):
    m = pc.shape[0]
    nc = seq // tq
    hb = SB_WIDTH // SB_HEAD_DIM
    return pl.pallas_call(
        functools.partial(_sb_kernel, tq=tq),
        grid=(batch, SB_HEADS, nc),
        in_specs=[
            pl.BlockSpec((tq, SB_HEAD_DIM), lambda b, h, c: (b * nc + c, h)),
            pl.BlockSpec((seq, SB_HEAD_DIM), lambda b, h, c: (b, hb + h)),
            pl.BlockSpec((seq, SB_HEAD_DIM), lambda b, h, c: (b, 2 * hb + h)),
            pl.BlockSpec((tq, SB_HEAD_DIM), lambda b, h, c: (b * nc + c, 3 * hb + h)),
        ],
        out_specs=pl.BlockSpec((tq, SB_HEAD_DIM), lambda b, h, c: (b * nc + c, h)),
        out_shape=jax.ShapeDtypeStruct((m, SB_WIDTH), BF16),
        compiler_params=pltpu.CompilerParams(
            dimension_semantics=("arbitrary", "arbitrary", "arbitrary"),
            vmem_limit_bytes=VMEM_LIMIT),
        name="sb",
    )(pc, pc, pc, pc)


def _merge_kernel(ya_ref, yb_ref, yc_ref, pg_ref, x_ref, wa_ref, wb_ref, wc_ref, bm_ref,
                  wo_ref, gp_ref, o_ref):
    merged = None
    for n, (y_ref, w_ref) in enumerate(((ya_ref, wa_ref), (yb_ref, wb_ref), (yc_ref, wc_ref))):
        cols = slice(n * D_MODEL, (n + 1) * D_MODEL)
        gates = _sigmoid(pg_ref[:, cols].astype(F32) + bm_ref[n:n + 1, :])
        term = gates * jnp.dot(y_ref[...], w_ref[...], preferred_element_type=F32)
        merged = term if merged is None else merged + term
    out = jnp.dot(merged.astype(BF16), wo_ref[...], preferred_element_type=F32)
    ms = jnp.mean(out * out, axis=-1, keepdims=True)
    o_ref[...] = x_ref[...] + out * lax.rsqrt(ms + EPS) * gp_ref[...]


def _merge(ya, yb, yc, pg, x, wa, wb, wc, b_merge, w_out, g_post, tm):
    m = x.shape[0]
    row = lambda width: pl.BlockSpec((tm, width), lambda i: (i, 0))
    full = lambda a: pl.BlockSpec(a.shape, lambda i: (0, 0))
    return pl.pallas_call(
        _merge_kernel,
        grid=(m // tm,),
        in_specs=[row(GM_WIDTH), row(SW_WIDTH), row(SB_WIDTH), row(PG_WIDTH), row(D_MODEL),
                  full(wa), full(wb), full(wc), full(b_merge), full(w_out), full(g_post)],
        out_specs=row(D_MODEL),
        out_shape=jax.ShapeDtypeStruct((m, D_MODEL), F32),
        compiler_params=pltpu.CompilerParams(
            dimension_semantics=("arbitrary",), vmem_limit_bytes=VMEM_LIMIT),
        name="merge",
    )(ya, yb, yc, pg, x, wa, wb, wc, b_merge, w_out, g_post)


def _rope_tables(seq):
    half = HEAD_DIM // 2
    freqs = ROPE_THETA ** (-jnp.arange(half, dtype=F32) / half)
    ang = jnp.arange(seq).astype(F32)[:, None] * freqs[None, :]
    cos = jnp.tile(jnp.cos(ang), (1, LANES // half))
    sin = jnp.tile(jnp.concatenate([-jnp.sin(ang), jnp.sin(ang)], axis=-1), (1, LANES // HEAD_DIM))
    return cos, sin


def kernel(x, w_in, gm_w_s, gm_b_s, gm_norm_gain, sw_sinks, w_branch_a, w_branch_b, w_branch_c,
           b_merge, w_out, g_pre, g_post):
    batch, seq, _ = x.shape
    depth = w_in.shape[0]
    cos_t, sin_t = _rope_tables(seq)
    xf = x.reshape(batch * seq, D_MODEL)
    for l in range(depth):
        pa, pb, pc, pg = _proj(xf, g_pre[l][None, :], w_in[l].astype(BF16), tm=512)
        ya = _gmlp(pa, gm_w_s[l], gm_b_s[l].T, gm_norm_gain[l][None, :], tm=512)
        yb = _swa(pb, sw_sinks[l], cos_t, sin_t, batch, seq, tq=512)
        yc = _sb(pc, batch, seq, tq=min(2048, seq))
        xf = _merge(ya, yb, yc, pg, xf, w_branch_a[l].astype(BF16), w_branch_b[l].astype(BF16),
                    w_branch_c[l].astype(BF16), b_merge[l], w_out[l].astype(BF16),
                    g_post[l][None, :], tm=512)
    return xf.reshape(batch, seq, D_MODEL)
```

```python
import jax
import jax.numpy as jnp
from jax import lax
from jax.experimental import pallas as pl
from jax.experimental.pallas import tpu as pltpu

F32 = jnp.float32
BF16 = jnp.bfloat16

D_MODEL = 1024
BLOCK = 128
EPS = 1e-6
NEG = -1e30
GM_GROUPS = 4
GM_WIDTH = 512
HEAD_DIM = 64
SW_HEADS = 8
SW_KV_HEADS = 2
SW_WIDTH = 512
SW_KV_WIDTH = 128
ROPE_THETA = 10000.0
SB_HEADS = 4
SB_HEAD_DIM = 128
SB_WIDTH = 512
N_BRANCH = 3
IN_WIDTH = 7936

V7X_VMEM_BYTES = 64 * 1024 * 1024
LANES = 128
VMEM_LIMIT = V7X_VMEM_BYTES - 8 * 1024 * 1024

ROWS_PROJ = 512
ROWS_GMLP = 512
ROWS_SWA = 512
ROWS_SB = 2048
ROWS_MERGE = 512
SB_GROUP = 8
SB_FAST_TILES = 3

PA_WIDTH = 3 * GM_WIDTH
PB_WIDTH = 2 * SW_WIDTH + 2 * SW_KV_WIDTH
PC_WIDTH = 4 * SB_WIDTH
PG_WIDTH = N_BRANCH * D_MODEL
PROJ_CHUNK = 256

SB_EXP_ZERO = 104.0
SB_STOP = SB_EXP_ZERO * 1.1
LOG2E = 1.4426950408889634


def _proj_chunks():
    segs = [
        (0, 1536, 0, 0, 1.0),
        (1536, 2048, 1, 0, HEAD_DIM ** -0.5),
        (2048, 2304, 1, 2 * SW_WIDTH, 1.0),
        (2304, 2816, 1, SW_WIDTH, 1.0),
        (2816, 3328, 2, 0, SB_HEAD_DIM ** -0.5 * LOG2E),
        (3328, 4864, 2, SB_WIDTH, 1.0),
        (4864, 7936, 3, 0, 1.0),
    ]
    out = []
    for lo, hi, idx, dst, scale in segs:
        for c in range(lo, hi, PROJ_CHUNK):
            out.append((c, idx, dst + c - lo, scale))
    return out


def _sigmoid(x):
    return 1.0 / (1.0 + jnp.exp(-x))


def _silu(x):
    return x * _sigmoid(x)


def _proj_kernel(x_ref, g_ref, w_ref, pa_ref, pb_ref, pc_ref, pg_ref):
    x = x_ref[...]
    ms = jnp.mean(x * x, axis=-1, keepdims=True)
    h = (x * lax.rsqrt(ms + EPS) * g_ref[...]).astype(BF16)
    outs = (pa_ref, pb_ref, pc_ref, pg_ref)
    for src, idx, dst, scale in _proj_chunks():
        r = jnp.dot(h, w_ref[:, src:src + PROJ_CHUNK], preferred_element_type=F32)
        if scale != 1.0:
            r = r * scale
        outs[idx][:, dst:dst + PROJ_CHUNK] = r.astype(BF16)


def _proj(x, g_pre, w_in):
    m = x.shape[0]
    tm = ROWS_PROJ
    return pl.pallas_call(
        _proj_kernel,
        grid=(m // tm,),
        in_specs=[
            pl.BlockSpec((tm, D_MODEL), lambda i: (i, 0)),
            pl.BlockSpec((1, D_MODEL), lambda i: (0, 0)),
            pl.BlockSpec((D_MODEL, IN_WIDTH), lambda i: (0, 0), pipeline_mode=pl.Buffered(1)),
        ],
        out_specs=[
            pl.BlockSpec((tm, PA_WIDTH), lambda i: (i, 0)),
            pl.BlockSpec((tm, PB_WIDTH), lambda i: (i, 0)),
            pl.BlockSpec((tm, PC_WIDTH), lambda i: (i, 0)),
            pl.BlockSpec((tm, PG_WIDTH), lambda i: (i, 0)),
        ],
        out_shape=[
            jax.ShapeDtypeStruct((m, PA_WIDTH), BF16),
            jax.ShapeDtypeStruct((m, PB_WIDTH), BF16),
            jax.ShapeDtypeStruct((m, PC_WIDTH), BF16),
            jax.ShapeDtypeStruct((m, PG_WIDTH), BF16),
        ],
        compiler_params=pltpu.CompilerParams(
            dimension_semantics=("arbitrary",), vmem_limit_bytes=VMEM_LIMIT),
        name="proj",
    )(x, g_pre, w_in)


def _gmlp_kernel(u_ref, v_ref, gate_ref, ws_ref, bs_ref, gv_ref, o_ref):
    row = lax.broadcasted_iota(jnp.int32, (BLOCK, BLOCK), 0)
    col = lax.broadcasted_iota(jnp.int32, (BLOCK, BLOCK), 1)
    causal = col <= row
    w = [jnp.where(causal, ws_ref[g], 0.0).astype(BF16) for g in range(GM_GROUPS)]
    for c in range(ROWS_GMLP // BLOCK):
        rows = slice(c * BLOCK, (c + 1) * BLOCK)
        v = v_ref[rows, :].astype(F32)
        mu = jnp.mean(v, axis=-1, keepdims=True)
        d = v - mu
        var = jnp.mean(d * d, axis=-1, keepdims=True)
        vn = (d * lax.rsqrt(var + EPS) * gv_ref[...]).astype(BF16)
        for g in range(GM_GROUPS):
            cols = slice(g * BLOCK, (g + 1) * BLOCK)
            mixed = jnp.dot(w[g], vn[:, cols], preferred_element_type=F32) + bs_ref[:, g:g + 1]
            y = u_ref[rows, cols].astype(F32) * mixed * _silu(gate_ref[rows, cols].astype(F32))
            o_ref[rows, cols] = y.astype(BF16)


def _gmlp(pa, w_s, b_s_t, g_v):
    m = pa.shape[0]
    tm = ROWS_GMLP
    return pl.pallas_call(
        _gmlp_kernel,
        grid=(m // tm,),
        in_specs=[
            pl.BlockSpec((tm, GM_WIDTH), lambda i: (i, 0)),
            pl.BlockSpec((tm, GM_WIDTH), lambda i: (i, 1)),
            pl.BlockSpec((tm, GM_WIDTH), lambda i: (i, 2)),
            pl.BlockSpec((GM_GROUPS, BLOCK, BLOCK), lambda i: (0, 0, 0)),
            pl.BlockSpec((BLOCK, GM_GROUPS), lambda i: (0, 0)),
            pl.BlockSpec((1, GM_WIDTH), lambda i: (0, 0)),
        ],
        out_specs=pl.BlockSpec((tm, GM_WIDTH), lambda i: (i, 0)),
        out_shape=jax.ShapeDtypeStruct((m, GM_WIDTH), BF16),
        compiler_params=pltpu.CompilerParams(
            dimension_semantics=("arbitrary",), vmem_limit_bytes=VMEM_LIMIT),
        name="gmlp",
    )(pa, pa, pa, w_s, b_s_t, g_v)


def _swa_kernel(sinks_ref, q_ref, gate_ref, k_ref, v_ref, cos_ref, sin_ref, o_ref,
                kprev_ref, vprev_ref):
    j = pl.program_id(1)

    @pl.when(j == 0)
    def _():
        kprev_ref[...] = jnp.zeros(kprev_ref.shape, BF16)
        vprev_ref[...] = jnp.zeros(vprev_ref.shape, BF16)

    lane = lax.broadcasted_iota(jnp.int32, (BLOCK, LANES), 1)
    qrow = lax.broadcasted_iota(jnp.int32, (BLOCK, LANES), 0)
    rot_fwd = (lane % HEAD_DIM) < (HEAD_DIM // 2)
    lo = lane < HEAD_DIM
    from_prev = lane > qrow
    zero = jnp.zeros((BLOCK, LANES), F32)
    no_prev = jnp.where(j == 0, NEG, 0.0)
    lo_ones = jnp.where(lo, 1.0, 0.0).astype(BF16)
    hi_ones = jnp.where(lo, 0.0, 1.0).astype(BF16)
    ones2 = jnp.concatenate([lo_ones, lo_ones, hi_ones, hi_ones], axis=0)

    def rope(x, cos, sin):
        rot = jnp.where(rot_fwd, pltpu.roll(x, LANES - HEAD_DIM // 2, 1),
                        pltpu.roll(x, HEAD_DIM // 2, 1))
        return x * cos + rot * sin

    def placed(x):
        xr = pltpu.roll(x, HEAD_DIM, 1)
        return [jnp.where(lo, x, zero).astype(BF16), jnp.where(lo, zero, xr).astype(BF16),
                jnp.where(lo, xr, zero).astype(BF16), jnp.where(lo, zero, x).astype(BF16)]

    kp = [kprev_ref[n] for n in range(2 * SW_KV_HEADS)]
    vp = [vprev_ref[n] for n in range(2 * SW_KV_HEADS)]
    for b in range(ROWS_SWA // BLOCK):
        rows = slice(b * BLOCK, (b + 1) * BLOCK)
        cos = cos_ref[rows, :]
        sin = sin_ref[rows, :]
        kc = placed(rope(k_ref[rows, :].astype(F32), cos, sin))
        vc = placed(v_ref[rows, :].astype(F32))
        for pr in range(SW_HEADS // 2):
            g = pr // 2
            cols = slice(pr * LANES, (pr + 1) * LANES)
            k2 = jnp.concatenate([kp[2 * g], kc[2 * g], kp[2 * g + 1], kc[2 * g + 1]], axis=0)
            v2 = jnp.concatenate([vp[2 * g], vc[2 * g], vp[2 * g + 1], vc[2 * g + 1]], axis=0)
            qp = rope(q_ref[rows, cols].astype(F32), cos, sin).astype(BF16)
            s = lax.dot_general(qp, k2, (((1,), (1,)), ((), ())),
                                preferred_element_type=F32)
            parts = []
            tops = []
            for hh in range(2):
                s_prev = s[:, 2 * hh * BLOCK:(2 * hh + 1) * BLOCK]
                s_cur = s[:, (2 * hh + 1) * BLOCK:(2 * hh + 2) * BLOCK]
                if b == 0:
                    s_prev = s_prev + no_prev
                sc = jnp.where(from_prev, s_prev, s_cur)
                m = jnp.max(sc, axis=-1, keepdims=True)
                p = jnp.exp(sc - m)
                tops.append(m)
                parts.append(jnp.where(from_prev, p, zero).astype(BF16))
                parts.append(jnp.where(from_prev, zero, p).astype(BF16))
            r = jnp.dot(jnp.concatenate(parts, axis=1), jnp.concatenate([v2, ones2], axis=1),
                        preferred_element_type=F32)
            sink = jnp.where(lo[0:1, :], sinks_ref[2 * pr], sinks_ref[2 * pr + 1])
            denom = r[:, LANES:] + jnp.exp(sink - jnp.where(lo, tops[0], tops[1]))
            o = r[:, :LANES] / denom
            o_ref[rows, cols] = (o * _silu(gate_ref[rows, cols].astype(F32))).astype(BF16)
        kp, vp = kc, vc
    for n in range(2 * SW_KV_HEADS):
        kprev_ref[n] = kp[n]
        vprev_ref[n] = vp[n]


def _swa(pb, sinks, cos_t, sin_t, batch, seq):
    m = pb.shape[0]
    tq = ROWS_SWA
    nj = seq // tq
    kcol = 2 * SW_WIDTH // SW_KV_WIDTH
    return pl.pallas_call(
        _swa_kernel,
        grid=(batch, nj),
        in_specs=[
            pl.BlockSpec(memory_space=pltpu.SMEM),
            pl.BlockSpec((tq, SW_WIDTH), lambda b, j: (b * nj + j, 0)),
            pl.BlockSpec((tq, SW_WIDTH), lambda b, j: (b * nj + j, 1)),
            pl.BlockSpec((tq, SW_KV_WIDTH), lambda b, j: (b * nj + j, kcol)),
            pl.BlockSpec((tq, SW_KV_WIDTH), lambda b, j: (b * nj + j, kcol + 1)),
            pl.BlockSpec((tq, LANES), lambda b, j: (j, 0)),
            pl.BlockSpec((tq, LANES), lambda b, j: (j, 0)),
        ],
        out_specs=pl.BlockSpec((tq, SW_WIDTH), lambda b, j: (b * nj + j, 0)),
        out_shape=jax.ShapeDtypeStruct((m, SW_WIDTH), BF16),
        scratch_shapes=[
            pltpu.VMEM((2 * SW_KV_HEADS, BLOCK, LANES), BF16),
            pltpu.VMEM((2 * SW_KV_HEADS, BLOCK, LANES), BF16),
        ],
        compiler_params=pltpu.CompilerParams(
            dimension_semantics=("arbitrary", "arbitrary"), vmem_limit_bytes=VMEM_LIMIT),
        name="swa",
    )(sinks, pb, pb, pb, pb, cos_t, sin_t)


def _neg_abs(x):
    return lax.bitcast_convert_type(
        lax.bitcast_convert_type(x, jnp.uint32) | jnp.uint32(0x80000000), F32)


def _sb_kernel(q_ref, k_ref, v_ref, gate_ref, o_ref, suf_ref, acc_ref, carry_ref):
    c = pl.program_id(2)
    row = lax.broadcasted_iota(jnp.int32, (BLOCK, BLOCK), 0)
    col = lax.broadcasted_iota(jnp.int32, (BLOCK, BLOCK), 1)
    before = col < row
    one_suffix = jnp.where(row >= col, 1.0, 0.0).astype(BF16)
    half = jnp.concatenate([one_suffix, jnp.ones((BLOCK, BLOCK), BF16)], axis=1)
    suf_ref[...] = jnp.concatenate([half, half], axis=0)
    groups = ROWS_SB // (SB_GROUP * BLOCK)
    contract_lanes = (((1,), (1,)), ((), ()))
    stop = SB_STOP * LOG2E

    def fail_of(zl):
        return jnp.maximum(zl, 0.0) + jnp.log2(1.0 + jnp.exp2(_neg_abs(zl)))

    def suffix_and_total(f):
        hi = lax.bitcast_convert_type(
            lax.bitcast_convert_type(f, jnp.uint32) & jnp.uint32(0xFFFF0000), F32)
        lo = f - hi
        r = jnp.dot(jnp.concatenate([hi.astype(BF16), lo.astype(BF16)], axis=1), suf_ref[...],
                    preferred_element_type=F32)
        return r[:, :BLOCK], r[:, BLOCK:]

    def one_tile(q, m, carry, acc, diag):
        k0 = pl.multiple_of(m * BLOCK, BLOCK)
        zl = lax.dot_general(q, k_ref[pl.ds(k0, BLOCK), :], contract_lanes,
                             preferred_element_type=F32)
        f = fail_of(zl)
        if diag:
            f = jnp.where(before, f, 0.0)
        cs, tot = suffix_and_total(f)
        a = jnp.exp2(zl - cs - carry)
        if diag:
            a = jnp.where(before, a, 0.0)
        acc = acc + jnp.dot(a.astype(BF16), v_ref[pl.ds(k0, BLOCK), :],
                            preferred_element_type=F32)
        return carry + tot, acc

    def fast_block(q, i):
        k0 = pl.multiple_of(jnp.maximum(i - (SB_FAST_TILES - 1), 0) * BLOCK, BLOCK)
        span = SB_FAST_TILES * BLOCK
        zl = lax.dot_general(q, k_ref[pl.ds(k0, span), :], contract_lanes,
                             preferred_element_type=F32)
        f = fail_of(zl)
        parts = [None] * SB_FAST_TILES
        carry = None
        for t in reversed(range(SB_FAST_TILES)):
            cols = slice(t * BLOCK, (t + 1) * BLOCK)
            diag = t == SB_FAST_TILES - 1
            ft = jnp.where(before, f[:, cols], 0.0) if diag else f[:, cols]
            cs, tot = suffix_and_total(ft)
            e = zl[:, cols] - cs
            a = jnp.exp2(e if carry is None else e - carry)
            if diag:
                a = jnp.where(before, a, 0.0)
            parts[t] = a.astype(BF16)
            carry = tot if carry is None else carry + tot
        acc = jnp.dot(jnp.concatenate(parts, axis=1), v_ref[pl.ds(k0, span), :],
                      preferred_element_type=F32)
        return carry, acc

    def rows_of(blk):
        return pl.ds(pl.multiple_of(blk * BLOCK, BLOCK), BLOCK)

    def finish(blk, acc):
        rows = rows_of(blk)
        o_ref[rows, :] = (acc * _silu(gate_ref[rows, :].astype(F32))).astype(BF16)

    def walk_block(blk, start_offset):
        q = q_ref[rows_of(blk), :]

        def cond(st):
            m, least, _, _ = st
            return jnp.logical_and(m >= 0, least < stop)

        def body(st):
            m, _, carry, acc = st
            carry, acc = one_tile(q, m, carry, acc, False)
            return m - 1, jnp.min(carry), carry, acc

        carry = carry_ref[blk]
        start = c * (ROWS_SB // BLOCK) + blk - start_offset
        _, _, _, acc = lax.while_loop(cond, body, (start, jnp.min(carry), carry, acc_ref[blk]))
        finish(blk, acc)

    def walk_group(gl):
        def walk(ib, _):
            walk_block(gl * SB_GROUP + ib, SB_FAST_TILES)
            return 0
        lax.fori_loop(0, SB_GROUP, walk, 0)

    def group(gl, least_prev):
        g0 = c * (ROWS_SB // BLOCK) + gl * SB_GROUP
        least = None
        for ib in range(SB_GROUP):
            blk = gl * SB_GROUP + ib
            carry, acc = fast_block(q_ref[rows_of(blk), :], g0 + ib)
            finish(blk, acc)
            carry_ref[blk] = carry
            acc_ref[blk] = acc
            least = carry if least is None else jnp.minimum(least, carry)

        @pl.when(jnp.min(least_prev) < stop)
        def _():
            walk_group(gl - 1)

        return least

    settled = jnp.full((BLOCK, BLOCK), 2.0 * stop, F32)
    least_last = lax.fori_loop(0, groups, group, settled)

    @pl.when(jnp.min(least_last) < stop)
    def _():
        walk_group(groups - 1)

    @pl.when(c == 0)
    def _():
        def redo(ib, _):
            carry, acc = one_tile(q_ref[rows_of(ib), :], ib, jnp.zeros((BLOCK, BLOCK), F32),
                                  jnp.zeros((BLOCK, SB_HEAD_DIM), F32), True)
            carry_ref[ib] = carry
            acc_ref[ib] = acc
            walk_block(ib, 1)
            return 0
        lax.fori_loop(0, SB_FAST_TILES - 1, redo, 0)


def _sb(pc, batch, seq):
    m = pc.shape[0]
    tq = ROWS_SB
    nc = seq // tq
    hb = SB_WIDTH // SB_HEAD_DIM
    return pl.pallas_call(
        _sb_kernel,
        grid=(batch, SB_HEADS, nc),
        in_specs=[
            pl.BlockSpec((tq, SB_HEAD_DIM), lambda b, h, c: (b * nc + c, h)),
            pl.BlockSpec((seq, SB_HEAD_DIM), lambda b, h, c: (b, hb + h)),
            pl.BlockSpec((seq, SB_HEAD_DIM), lambda b, h, c: (b, 2 * hb + h)),
            pl.BlockSpec((tq, SB_HEAD_DIM), lambda b, h, c: (b * nc + c, 3 * hb + h)),
        ],
        out_specs=pl.BlockSpec((tq, SB_HEAD_DIM), lambda b, h, c: (b * nc + c, h)),
        out_shape=jax.ShapeDtypeStruct((m, SB_WIDTH), BF16),
        scratch_shapes=[
            pltpu.VMEM((2 * BLOCK, 2 * BLOCK), BF16),
            pltpu.VMEM((ROWS_SB // BLOCK, BLOCK, SB_HEAD_DIM), F32),
            pltpu.VMEM((ROWS_SB // BLOCK, BLOCK, BLOCK), F32),
        ],
        compiler_params=pltpu.CompilerParams(
            dimension_semantics=("arbitrary", "arbitrary", "arbitrary"),
            vmem_limit_bytes=VMEM_LIMIT),
        name="sb",
    )(pc, pc, pc, pc)


def _merge_kernel(ya_ref, yb_ref, yc_ref, pg_ref, x_ref, wa_ref, wb_ref, wc_ref, bm_ref,
                  wo_ref, gp_ref, o_ref):
    merged = None
    for n, (y_ref, w_ref) in enumerate(((ya_ref, wa_ref), (yb_ref, wb_ref), (yc_ref, wc_ref))):
        cols = slice(n * D_MODEL, (n + 1) * D_MODEL)
        gates = _sigmoid(pg_ref[:, cols].astype(F32) + bm_ref[n:n + 1, :])
        term = gates * jnp.dot(y_ref[...], w_ref[...], preferred_element_type=F32)
        merged = term if merged is None else merged + term
    out = jnp.dot(merged.astype(BF16), wo_ref[...], preferred_element_type=F32)
    ms = jnp.mean(out * out, axis=-1, keepdims=True)
    o_ref[...] = x_ref[...] + out * lax.rsqrt(ms + EPS) * gp_ref[...]


def _merge(ya, yb, yc, pg, x, wa, wb, wc, b_merge, w_out, g_post):
    m = x.shape[0]
    tm = ROWS_MERGE
    row = lambda width: pl.BlockSpec((tm, width), lambda i: (i, 0))
    full = lambda a: pl.BlockSpec(a.shape, lambda i: (0, 0))
    return pl.pallas_call(
        _merge_kernel,
        grid=(m // tm,),
        in_specs=[row(GM_WIDTH), row(SW_WIDTH), row(SB_WIDTH), row(PG_WIDTH), row(D_MODEL),
                  full(wa), full(wb), full(wc), full(b_merge), full(w_out), full(g_post)],
        out_specs=row(D_MODEL),
        out_shape=jax.ShapeDtypeStruct((m, D_MODEL), F32),
        compiler_params=pltpu.CompilerParams(
            dimension_semantics=("arbitrary",), vmem_limit_bytes=VMEM_LIMIT),
        name="merge",
    )(ya, yb, yc, pg, x, wa, wb, wc, b_merge, w_out, g_post)


def _rope_tables(seq):
    half = HEAD_DIM // 2
    freqs = ROPE_THETA ** (-jnp.arange(half, dtype=F32) / half)
    ang = jnp.arange(seq).astype(F32)[:, None] * freqs[None, :]
    cos = jnp.tile(jnp.cos(ang), (1, LANES // half))
    sin = jnp.tile(jnp.concatenate([-jnp.sin(ang), jnp.sin(ang)], axis=-1), (1, LANES // HEAD_DIM))
    return cos, sin


def kernel(x, w_in, gm_w_s, gm_b_s, gm_norm_gain, sw_sinks, w_branch_a, w_branch_b, w_branch_c,
           b_merge, w_out, g_pre, g_post):
    batch, seq, _ = x.shape
    assert seq % ROWS_SB == 0 and (batch * seq) % ROWS_PROJ == 0
    depth = w_in.shape[0]
    cos_t, sin_t = _rope_tables(seq)
    xf = x.reshape(batch * seq, D_MODEL)
    for l in range(depth):
        pa, pb, pc, pg = _proj(xf, g_pre[l][None, :], w_in[l].astype(BF16))
        ya = _gmlp(pa, gm_w_s[l], gm_b_s[l].T, gm_norm_gain[l][None, :])
        yb = _swa(pb, sw_sinks[l], cos_t, sin_t, batch, seq)
        yc = _sb(pc, batch, seq)
        xf = _merge(ya, yb, yc, pg, xf, w_branch_a[l].astype(BF16), w_branch_b[l].astype(BF16),
                    w_branch_c[l].astype(BF16), b_merge[l], w_out[l].astype(BF16),
                    g_post[l][None, :])
    return xf.reshape(batch, seq, D_MODEL)
```

```python
import jax
import jax.numpy as jnp
from jax import lax
from jax.experimental import pallas as pl
from jax.experimental.pallas import tpu as pltpu

F32 = jnp.float32
BF16 = jnp.bfloat16

D_MODEL = 1024
BLOCK = 128
EPS = 1e-6
NEG = -1e30
GM_GROUPS = 4
GM_WIDTH = 512
HEAD_DIM = 64
SW_HEADS = 8
SW_KV_HEADS = 2
SW_WIDTH = 512
SW_KV_WIDTH = 128
ROPE_THETA = 10000.0
SB_HEADS = 4
SB_HEAD_DIM = 128
SB_WIDTH = 512
N_BRANCH = 3
IN_WIDTH = 7936

V7X_VMEM_BYTES = 64 * 1024 * 1024
LANES = 128
VMEM_LIMIT = V7X_VMEM_BYTES - 8 * 1024 * 1024

ROWS_PROJ = 512
ROWS_GMLP = 512
ROWS_SWA = 512
ROWS_SB = 2048
ROWS_MERGE = 512
SB_GROUP = 8
SB_FAST_TILES = 3

PA_WIDTH = 3 * GM_WIDTH
PB_WIDTH = 2 * SW_WIDTH + 2 * SW_KV_WIDTH
PC_WIDTH = 4 * SB_WIDTH
PG_WIDTH = N_BRANCH * D_MODEL
PROJ_CHUNK = 256

SB_EXP_ZERO = 104.0
SB_STOP = SB_EXP_ZERO * 1.1
LOG2E = 1.4426950408889634


def _proj_chunks():
    segs = [
        (0, 1536, 0, 0, 1.0),
        (1536, 2048, 1, 0, HEAD_DIM ** -0.5),
        (2048, 2304, 1, 2 * SW_WIDTH, 1.0),
        (2304, 2816, 1, SW_WIDTH, 1.0),
        (2816, 3328, 2, 0, SB_HEAD_DIM ** -0.5 * LOG2E),
        (3328, 4864, 2, SB_WIDTH, 1.0),
        (4864, 7936, 3, 0, 1.0),
    ]
    out = []
    for lo, hi, idx, dst, scale in segs:
        for c in range(lo, hi, PROJ_CHUNK):
            out.append((c, idx, dst + c - lo, scale))
    return out


def _sigmoid(x):
    return 1.0 / (1.0 + jnp.exp(-x))


def _silu(x):
    return x * _sigmoid(x)


def _proj_kernel(x_ref, g_ref, w_ref, pa_ref, pb_ref, pc_ref, pg_ref):
    x = x_ref[...]
    ms = jnp.mean(x * x, axis=-1, keepdims=True)
    h = (x * lax.rsqrt(ms + EPS) * g_ref[...]).astype(BF16)
    outs = (pa_ref, pb_ref, pc_ref, pg_ref)
    for src, idx, dst, scale in _proj_chunks():
        r = jnp.dot(h, w_ref[:, src:src + PROJ_CHUNK], preferred_element_type=F32)
        if scale != 1.0:
            r = r * scale
        outs[idx][:, dst:dst + PROJ_CHUNK] = r.astype(BF16)


def _proj(x, g_pre, w_in):
    m = x.shape[0]
    tm = ROWS_PROJ
    return pl.pallas_call(
        _proj_kernel,
        grid=(m // tm,),
        in_specs=[
            pl.BlockSpec((tm, D_MODEL), lambda i: (i, 0)),
            pl.BlockSpec((1, D_MODEL), lambda i: (0, 0)),
            pl.BlockSpec((D_MODEL, IN_WIDTH), lambda i: (0, 0), pipeline_mode=pl.Buffered(1)),
        ],
        out_specs=[
            pl.BlockSpec((tm, PA_WIDTH), lambda i: (i, 0)),
            pl.BlockSpec((tm, PB_WIDTH), lambda i: (i, 0)),
            pl.BlockSpec((tm, PC_WIDTH), lambda i: (i, 0)),
            pl.BlockSpec((tm, PG_WIDTH), lambda i: (i, 0)),
        ],
        out_shape=[
            jax.ShapeDtypeStruct((m, PA_WIDTH), BF16),
            jax.ShapeDtypeStruct((m, PB_WIDTH), BF16),
            jax.ShapeDtypeStruct((m, PC_WIDTH), BF16),
            jax.ShapeDtypeStruct((m, PG_WIDTH), BF16),
        ],
        compiler_params=pltpu.CompilerParams(
            dimension_semantics=("arbitrary",), vmem_limit_bytes=VMEM_LIMIT),
        name="proj",
    )(x, g_pre, w_in)


def _gmlp_kernel(u_ref, v_ref, gate_ref, ws_ref, bs_ref, gv_ref, o_ref):
    row = lax.broadcasted_iota(jnp.int32, (BLOCK, BLOCK), 0)
    col = lax.broadcasted_iota(jnp.int32, (BLOCK, BLOCK), 1)
    causal = col <= row
    w = [jnp.where(causal, ws_ref[g], 0.0).astype(BF16) for g in range(GM_GROUPS)]
    for c in range(ROWS_GMLP // BLOCK):
        rows = slice(c * BLOCK, (c + 1) * BLOCK)
        v = v_ref[rows, :].astype(F32)
        mu = jnp.mean(v, axis=-1, keepdims=True)
        d = v - mu
        var = jnp.mean(d * d, axis=-1, keepdims=True)
        vn = (d * lax.rsqrt(var + EPS) * gv_ref[...]).astype(BF16)
        for g in range(GM_GROUPS):
            cols = slice(g * BLOCK, (g + 1) * BLOCK)
            mixed = jnp.dot(w[g], vn[:, cols], preferred_element_type=F32) + bs_ref[:, g:g + 1]
            y = u_ref[rows, cols].astype(F32) * mixed * _silu(gate_ref[rows, cols].astype(F32))
            o_ref[rows, cols] = y.astype(BF16)


def _gmlp(pa, w_s, b_s_t, g_v):
    m = pa.shape[0]
    tm = ROWS_GMLP
    return pl.pallas_call(
        _gmlp_kernel,
        grid=(m // tm,),
        in_specs=[
            pl.BlockSpec((tm, GM_WIDTH), lambda i: (i, 0)),
            pl.BlockSpec((tm, GM_WIDTH), lambda i: (i, 1)),
            pl.BlockSpec((tm, GM_WIDTH), lambda i: (i, 2)),
            pl.BlockSpec((GM_GROUPS, BLOCK, BLOCK), lambda i: (0, 0, 0)),
            pl.BlockSpec((BLOCK, GM_GROUPS), lambda i: (0, 0)),
            pl.BlockSpec((1, GM_WIDTH), lambda i: (0, 0)),
        ],
        out_specs=pl.BlockSpec((tm, GM_WIDTH), lambda i: (i, 0)),
        out_shape=jax.ShapeDtypeStruct((m, GM_WIDTH), BF16),
        compiler_params=pltpu.CompilerParams(
            dimension_semantics=("arbitrary",), vmem_limit_bytes=VMEM_LIMIT),
        name="gmlp",
    )(pa, pa, pa, w_s, b_s_t, g_v)


def _swa_kernel(sinks_ref, q_ref, gate_ref, k_ref, v_ref, cos_ref, sin_ref, o_ref,
                kprev_ref, vprev_ref, sc_ref):
    j = pl.program_id(1)

    @pl.when(j == 0)
    def _():
        kprev_ref[...] = jnp.zeros(kprev_ref.shape, BF16)
        vprev_ref[...] = jnp.zeros(vprev_ref.shape, BF16)

    lane = lax.broadcasted_iota(jnp.int32, (BLOCK, LANES), 1)
    qrow = lax.broadcasted_iota(jnp.int32, (BLOCK, LANES), 0)
    rot_fwd = (lane % HEAD_DIM) < (HEAD_DIM // 2)
    lo = lane < HEAD_DIM
    from_prev = lane > qrow
    zero = jnp.zeros((BLOCK, LANES), F32)
    no_prev = jnp.where(j == 0, NEG, 0.0)
    lo_ones = jnp.where(lo, 1.0, 0.0).astype(BF16)
    hi_ones = jnp.where(lo, 0.0, 1.0).astype(BF16)
    ones2 = jnp.concatenate([lo_ones, lo_ones, hi_ones, hi_ones], axis=0)

    def rope(x, cos, sin):
        rot = jnp.where(rot_fwd, pltpu.roll(x, LANES - HEAD_DIM // 2, 1),
                        pltpu.roll(x, HEAD_DIM // 2, 1))
        return x * cos + rot * sin

    def placed(x):
        xr = pltpu.roll(x, HEAD_DIM, 1)
        return [jnp.where(lo, x, zero).astype(BF16), jnp.where(lo, zero, xr).astype(BF16),
                jnp.where(lo, xr, zero).astype(BF16), jnp.where(lo, zero, x).astype(BF16)]

    kp = [kprev_ref[n] for n in range(2 * SW_KV_HEADS)]
    vp = [vprev_ref[n] for n in range(2 * SW_KV_HEADS)]
    nblk = ROWS_SWA // BLOCK
    v2s = {}
    for b in range(nblk):
        rows = slice(b * BLOCK, (b + 1) * BLOCK)
        cos = cos_ref[rows, :]
        sin = sin_ref[rows, :]
        kc = placed(rope(k_ref[rows, :].astype(F32), cos, sin))
        vc = placed(v_ref[rows, :].astype(F32))
        for g in range(SW_KV_HEADS):
            v2 = jnp.concatenate([vp[2 * g], vc[2 * g], vp[2 * g + 1], vc[2 * g + 1]], axis=0)
            v2s[(b, g)] = jnp.concatenate([v2, ones2], axis=1)
        for pr in range(SW_HEADS // 2):
            g = pr // 2
            cols = slice(pr * LANES, (pr + 1) * LANES)
            k2 = jnp.concatenate([kp[2 * g], kc[2 * g], kp[2 * g + 1], kc[2 * g + 1]], axis=0)
            qp = rope(q_ref[rows, cols].astype(F32), cos, sin).astype(BF16)
            s = lax.dot_general(qp, k2, (((1,), (1,)), ((), ())),
                                preferred_element_type=F32)
            for hh in range(2):
                s_prev = s[:, 2 * hh * BLOCK:(2 * hh + 1) * BLOCK]
                s_cur = s[:, (2 * hh + 1) * BLOCK:(2 * hh + 2) * BLOCK]
                if b == 0:
                    s_prev = s_prev + no_prev
                sc_ref[b, pr, :, hh * BLOCK:(hh + 1) * BLOCK] = jnp.where(from_prev, s_prev, s_cur)
        kp, vp = kc, vc
    for n in range(2 * SW_KV_HEADS):
        kprev_ref[n] = kp[n]
        vprev_ref[n] = vp[n]
    for b in range(nblk):
        rows = slice(b * BLOCK, (b + 1) * BLOCK)
        for pr in range(SW_HEADS // 2):
            g = pr // 2
            cols = slice(pr * LANES, (pr + 1) * LANES)
            parts = []
            tops = []
            for hh in range(2):
                sc = sc_ref[b, pr, :, hh * BLOCK:(hh + 1) * BLOCK]
                m = jnp.max(sc, axis=-1, keepdims=True)
                p = jnp.exp(sc - m)
                tops.append(m)
                parts.append(jnp.where(from_prev, p, zero).astype(BF16))
                parts.append(jnp.where(from_prev, zero, p).astype(BF16))
            r = jnp.dot(jnp.concatenate(parts, axis=1), v2s[(b, g)],
                        preferred_element_type=F32)
            sink = jnp.where(lo[0:1, :], sinks_ref[2 * pr], sinks_ref[2 * pr + 1])
            denom = r[:, LANES:] + jnp.exp(sink - jnp.where(lo, tops[0], tops[1]))
            o = r[:, :LANES] / denom
            o_ref[rows, cols] = (o * _silu(gate_ref[rows, cols].astype(F32))).astype(BF16)


def _swa(pb, sinks, cos_t, sin_t, batch, seq):
    m = pb.shape[0]
    tq = ROWS_SWA
    nj = seq // tq
    kcol = 2 * SW_WIDTH // SW_KV_WIDTH
    return pl.pallas_call(
        _swa_kernel,
        grid=(batch, nj),
        in_specs=[
            pl.BlockSpec(memory_space=pltpu.SMEM),
            pl.BlockSpec((tq, SW_WIDTH), lambda b, j: (b * nj + j, 0)),
            pl.BlockSpec((tq, SW_WIDTH), lambda b, j: (b * nj + j, 1)),
            pl.BlockSpec((tq, SW_KV_WIDTH), lambda b, j: (b * nj + j, kcol)),
            pl.BlockSpec((tq, SW_KV_WIDTH), lambda b, j: (b * nj + j, kcol + 1)),
            pl.BlockSpec((tq, LANES), lambda b, j: (j, 0)),
            pl.BlockSpec((tq, LANES), lambda b, j: (j, 0)),
        ],
        out_specs=pl.BlockSpec((tq, SW_WIDTH), lambda b, j: (b * nj + j, 0)),
        out_shape=jax.ShapeDtypeStruct((m, SW_WIDTH), BF16),
        scratch_shapes=[
            pltpu.VMEM((2 * SW_KV_HEADS, BLOCK, LANES), BF16),
            pltpu.VMEM((2 * SW_KV_HEADS, BLOCK, LANES), BF16),
            pltpu.VMEM((ROWS_SWA // BLOCK, SW_HEADS // 2, BLOCK, 2 * BLOCK), F32),
        ],
        compiler_params=pltpu.CompilerParams(
            dimension_semantics=("arbitrary", "arbitrary"), vmem_limit_bytes=VMEM_LIMIT),
        name="swa",
    )(sinks, pb, pb, pb, pb, cos_t, sin_t)


def _sb_kernel(q_ref, k_ref, v_ref, gate_ref, o_ref, suf_ref, acc_ref, carry_ref,
               z_ref, hl_ref, a_ref):
    c = pl.program_id(2)
    row = lax.broadcasted_iota(jnp.int32, (BLOCK, BLOCK), 0)
    col = lax.broadcasted_iota(jnp.int32, (BLOCK, BLOCK), 1)
    before = col < row
    one_suffix = jnp.where(row >= col, 1.0, 0.0).astype(BF16)
    half = jnp.concatenate([one_suffix, jnp.ones((BLOCK, BLOCK), BF16)], axis=1)
    suf_ref[...] = jnp.concatenate([half, half], axis=0)
    groups = ROWS_SB // (SB_GROUP * BLOCK)
    contract_lanes = (((1,), (1,)), ((), ()))
    stop = SB_STOP * LOG2E

    def fail_of(zl):
        return jnp.maximum(zl, 0.0) + jnp.log2(1.0 + jnp.exp2(-jnp.abs(zl)))

    def split(f):
        hi = f.astype(BF16)
        lo = (f - hi.astype(F32)).astype(BF16)
        return jnp.concatenate([hi, lo], axis=1)

    def suffix_and_total(hl):
        r = jnp.dot(hl, suf_ref[...], preferred_element_type=F32)
        return r[:, :BLOCK], r[:, BLOCK:]

    def one_tile(q, m, carry, acc, diag):
        k0 = pl.multiple_of(m * BLOCK, BLOCK)
        zl = lax.dot_general(q, k_ref[pl.ds(k0, BLOCK), :], contract_lanes,
                             preferred_element_type=F32)
        f = fail_of(zl)
        if diag:
            f = jnp.where(before, f, 0.0)
        cs, tot = suffix_and_total(split(f))
        a = jnp.exp2(zl - cs - carry)
        if diag:
            a = jnp.where(before, a, 0.0)
        acc = acc + jnp.dot(a.astype(BF16), v_ref[pl.ds(k0, BLOCK), :],
                            preferred_element_type=F32)
        return carry + tot, acc

    def rows_of(blk):
        return pl.ds(pl.multiple_of(blk * BLOCK, BLOCK), BLOCK)

    def finish(blk, acc):
        rows = rows_of(blk)
        o_ref[rows, :] = (acc * _silu(gate_ref[rows, :].astype(F32))).astype(BF16)

    def walk_block(blk, start_offset):
        q = q_ref[rows_of(blk), :]

        def cond(st):
            m, least, _, _ = st
            return jnp.logical_and(m >= 0, least < stop)

        def body(st):
            m, _, carry, acc = st
            carry, acc = one_tile(q, m, carry, acc, False)
            return m - 1, jnp.min(carry), carry, acc

        carry = carry_ref[blk]
        start = c * (ROWS_SB // BLOCK) + blk - start_offset
        _, _, _, acc = lax.while_loop(cond, body, (start, jnp.min(carry), carry, acc_ref[blk]))
        finish(blk, acc)

    def walk_group(gl):
        def walk(ib, _):
            walk_block(gl * SB_GROUP + ib, SB_FAST_TILES)
            return 0
        lax.fori_loop(0, SB_GROUP, walk, 0)

    def group(gl, least_prev):
        g0 = c * (ROWS_SB // BLOCK) + gl * SB_GROUP
        span = SB_FAST_TILES * BLOCK
        diag = SB_FAST_TILES - 1
        spans = []
        for ib in range(SB_GROUP):
            blk = gl * SB_GROUP + ib
            k0 = pl.multiple_of(jnp.maximum(g0 + ib - diag, 0) * BLOCK, BLOCK)
            spans.append(pl.ds(k0, span))
            zl = lax.dot_general(q_ref[rows_of(blk), :], k_ref[spans[ib], :], contract_lanes,
                                 preferred_element_type=F32)
            f = fail_of(zl)
            z_ref[ib] = zl
            for t in range(SB_FAST_TILES):
                cols = slice(t * BLOCK, (t + 1) * BLOCK)
                hl_ref[ib, t] = split(jnp.where(before, f[:, cols], 0.0) if t == diag
                                      else f[:, cols])
        least = None
        for ib in range(SB_GROUP):
            carry = None
            for t in reversed(range(SB_FAST_TILES)):
                cols = slice(t * BLOCK, (t + 1) * BLOCK)
                cs, tot = suffix_and_total(hl_ref[ib, t])
                e = z_ref[ib, :, cols] - cs
                a = jnp.exp2(e if carry is None else e - carry)
                if t == diag:
                    a = jnp.where(before, a, 0.0)
                a_ref[ib, :, cols] = a.astype(BF16)
                carry = tot if carry is None else carry + tot
            carry_ref[gl * SB_GROUP + ib] = carry
            least = carry if least is None else jnp.minimum(least, carry)
        for ib in range(SB_GROUP):
            blk = gl * SB_GROUP + ib
            acc = jnp.dot(a_ref[ib], v_ref[spans[ib], :], preferred_element_type=F32)
            finish(blk, acc)
            acc_ref[blk] = acc

        @pl.when(jnp.min(least_prev) < stop)
        def _():
            walk_group(gl - 1)

        return least

    settled = jnp.full((BLOCK, BLOCK), 2.0 * stop, F32)
    least_last = lax.fori_loop(0, groups, group, settled)

    @pl.when(jnp.min(least_last) < stop)
    def _():
        walk_group(groups - 1)

    @pl.when(c == 0)
    def _():
        def redo(ib, _):
            carry, acc = one_tile(q_ref[rows_of(ib), :], ib, jnp.zeros((BLOCK, BLOCK), F32),
                                  jnp.zeros((BLOCK, SB_HEAD_DIM), F32), True)
            carry_ref[ib] = carry
            acc_ref[ib] = acc
            walk_block(ib, 1)
            return 0
        lax.fori_loop(0, SB_FAST_TILES - 1, redo, 0)


def _sb(pc, batch, seq):
    m = pc.shape[0]
    tq = ROWS_SB
    nc = seq // tq
    hb = SB_WIDTH // SB_HEAD_DIM
    return pl.pallas_call(
        _sb_kernel,
        grid=(batch, SB_HEADS, nc),
        in_specs=[
            pl.BlockSpec((tq, SB_HEAD_DIM), lambda b, h, c: (b * nc + c, h)),
            pl.BlockSpec((seq, SB_HEAD_DIM), lambda b, h, c: (b, hb + h)),
            pl.BlockSpec((seq, SB_HEAD_DIM), lambda b, h, c: (b, 2 * hb + h)),
            pl.BlockSpec((tq, SB_HEAD_DIM), lambda b, h, c: (b * nc + c, 3 * hb + h)),
        ],
        out_specs=pl.BlockSpec((tq, SB_HEAD_DIM), lambda b, h, c: (b * nc + c, h)),
        out_shape=jax.ShapeDtypeStruct((m, SB_WIDTH), BF16),
        scratch_shapes=[
            pltpu.VMEM((2 * BLOCK, 2 * BLOCK), BF16),
            pltpu.VMEM((ROWS_SB // BLOCK, BLOCK, SB_HEAD_DIM), F32),
            pltpu.VMEM((ROWS_SB // BLOCK, BLOCK, BLOCK), F32),
            pltpu.VMEM((SB_GROUP, BLOCK, SB_FAST_TILES * BLOCK), F32),
            pltpu.VMEM((SB_GROUP, SB_FAST_TILES, BLOCK, 2 * BLOCK), BF16),
            pltpu.VMEM((SB_GROUP, BLOCK, SB_FAST_TILES * BLOCK), BF16),
        ],
        compiler_params=pltpu.CompilerParams(
            dimension_semantics=("arbitrary", "arbitrary", "arbitrary"),
            vmem_limit_bytes=VMEM_LIMIT),
        name="sb",
    )(pc, pc, pc, pc)


def _merge_kernel(ya_ref, yb_ref, yc_ref, pg_ref, x_ref, wa_ref, wb_ref, wc_ref, bm_ref,
                  wo_ref, gp_ref, o_ref):
    merged = None
    for n, (y_ref, w_ref) in enumerate(((ya_ref, wa_ref), (yb_ref, wb_ref), (yc_ref, wc_ref))):
        cols = slice(n * D_MODEL, (n + 1) * D_MODEL)
        gates = _sigmoid(pg_ref[:, cols].astype(F32) + bm_ref[n:n + 1, :])
        term = gates * jnp.dot(y_ref[...], w_ref[...], preferred_element_type=F32)
        merged = term if merged is None else merged + term
    out = jnp.dot(merged.astype(BF16), wo_ref[...], preferred_element_type=F32)
    ms = jnp.mean(out * out, axis=-1, keepdims=True)
    o_ref[...] = x_ref[...] + out * lax.rsqrt(ms + EPS) * gp_ref[...]


def _merge(ya, yb, yc, pg, x, wa, wb, wc, b_merge, w_out, g_post):
    m = x.shape[0]
    tm = ROWS_MERGE
    row = lambda width: pl.BlockSpec((tm, width), lambda i: (i, 0))
    full = lambda a: pl.BlockSpec(a.shape, lambda i: (0, 0))
    return pl.pallas_call(
        _merge_kernel,
        grid=(m // tm,),
        in_specs=[row(GM_WIDTH), row(SW_WIDTH), row(SB_WIDTH), row(PG_WIDTH), row(D_MODEL),
                  full(wa), full(wb), full(wc), full(b_merge), full(w_out), full(g_post)],
        out_specs=row(D_MODEL),
        out_shape=jax.ShapeDtypeStruct((m, D_MODEL), F32),
        compiler_params=pltpu.CompilerParams(
            dimension_semantics=("arbitrary",), vmem_limit_bytes=VMEM_LIMIT),
        name="merge",
    )(ya, yb, yc, pg, x, wa, wb, wc, b_merge, w_out, g_post)


def _rope_tables(seq):
    half = HEAD_DIM // 2
    freqs = ROPE_THETA ** (-jnp.arange(half, dtype=F32) / half)
    ang = jnp.arange(seq).astype(F32)[:, None] * freqs[None, :]
    cos = jnp.tile(jnp.cos(ang), (1, LANES // half))
    sin = jnp.tile(jnp.concatenate([-jnp.sin(ang), jnp.sin(ang)], axis=-1), (1, LANES // HEAD_DIM))
    return cos, sin


def kernel(x, w_in, gm_w_s, gm_b_s, gm_norm_gain, sw_sinks, w_branch_a, w_branch_b, w_branch_c,
           b_merge, w_out, g_pre, g_post):
    batch, seq, _ = x.shape
    assert seq % ROWS_SB == 0 and (batch * seq) % ROWS_PROJ == 0
    depth = w_in.shape[0]
    cos_t, sin_t = _rope_tables(seq)
    xf = x.reshape(batch * seq, D_MODEL)
    for l in range(depth):
        pa, pb, pc, pg = _proj(xf, g_pre[l][None, :], w_in[l].astype(BF16))
        ya = _gmlp(pa, gm_w_s[l], gm_b_s[l].T, gm_norm_gain[l][None, :])
        yb = _swa(pb, sw_sinks[l], cos_t, sin_t, batch, seq)
        yc = _sb(pc, batch, seq)
        xf = _merge(ya, yb, yc, pg, xf, w_branch_a[l].astype(BF16), w_branch_b[l].astype(BF16),
                    w_branch_c[l].astype(BF16), b_merge[l], w_out[l].astype(BF16),
                    g_post[l][None, :])
    return xf.reshape(batch, seq, D_MODEL)
```

```python
import jax
import jax.numpy as jnp
from jax import lax
from jax.experimental import pallas as pl
from jax.experimental.pallas import tpu as pltpu

F32 = jnp.float32
BF16 = jnp.bfloat16

D_MODEL = 1024
BLOCK = 128
EPS = 1e-6
NEG = -1e30
GM_GROUPS = 4
GM_WIDTH = 512
HEAD_DIM = 64
SW_HEADS = 8
SW_KV_HEADS = 2
SW_WIDTH = 512
SW_KV_WIDTH = 128
ROPE_THETA = 10000.0
SB_HEADS = 4
SB_HEAD_DIM = 128
SB_WIDTH = 512
N_BRANCH = 3
IN_WIDTH = 7936

V7X_VMEM_BYTES = 64 * 1024 * 1024
LANES = 128
VMEM_LIMIT = V7X_VMEM_BYTES - 8 * 1024 * 1024

ROWS_PROJ = 512
ROWS_SWA = 512
ROWS_SB = 2048
ROWS_MERGE = 512
SB_GROUP = 8
SB_FAST_TILES = 3

PA_WIDTH = 3 * GM_WIDTH
PB_WIDTH = 2 * SW_WIDTH + 2 * SW_KV_WIDTH
PC_WIDTH = 4 * SB_WIDTH
PG_WIDTH = N_BRANCH * D_MODEL
PROJ_CHUNK = 256

SB_EXP_ZERO = 104.0
SB_STOP = SB_EXP_ZERO * 1.1
LOG2E = 1.4426950408889634


def _proj_chunks():
    segs = [
        (0, 1536, 0, 0, 1.0),
        (1536, 2048, 1, 0, HEAD_DIM ** -0.5),
        (2048, 2304, 1, 2 * SW_WIDTH, 1.0),
        (2304, 2816, 1, SW_WIDTH, 1.0),
        (2816, 3328, 2, 0, SB_HEAD_DIM ** -0.5 * LOG2E),
        (3328, 4864, 2, SB_WIDTH, 1.0),
        (4864, 7936, 3, 0, 1.0),
    ]
    out = []
    for lo, hi, idx, dst, scale in segs:
        for c in range(lo, hi, PROJ_CHUNK):
            out.append((c, idx, dst + c - lo, scale))
    return out


def _sigmoid(x):
    return 1.0 / (1.0 + jnp.exp(-x))


def _silu(x):
    return x * _sigmoid(x)


def _rope(x, cos, sin):
    lane = lax.broadcasted_iota(jnp.int32, x.shape, 1)
    rot_fwd = (lane % HEAD_DIM) < (HEAD_DIM // 2)
    rot = jnp.where(rot_fwd, pltpu.roll(x, LANES - HEAD_DIM // 2, 1),
                    pltpu.roll(x, HEAD_DIM // 2, 1))
    return x * cos + rot * sin


def _front_kernel(x_ref, g_ref, w_ref, ws_ref, bs_ref, gv_ref, bm_ref, cos_ref, sin_ref,
                  ya_ref, pb_ref, pc_ref, pg_ref, pa_ref):
    x = x_ref[...]
    ms = jnp.mean(x * x, axis=-1, keepdims=True)
    h = (x * lax.rsqrt(ms + EPS) * g_ref[...]).astype(BF16)
    outs = (pa_ref, pb_ref, pc_ref, pg_ref)
    cos = cos_ref[...]
    sin = sin_ref[...]

    def emit(chunk):
        src, idx, dst, scale = chunk
        r = jnp.dot(h, w_ref[:, src:src + PROJ_CHUNK], preferred_element_type=F32)
        if scale != 1.0:
            r = r * scale
        if idx == 0:
            pa_ref[:, dst:dst + PROJ_CHUNK] = r
            return
        if idx == 1 and dst < SW_WIDTH:
            r = jnp.concatenate([_rope(r[:, :LANES], cos, sin), _rope(r[:, LANES:], cos, sin)],
                                axis=1)
        elif idx == 1 and dst == 2 * SW_WIDTH:
            r = jnp.concatenate([_rope(r[:, :LANES], cos, sin), r[:, LANES:]], axis=1)
        elif idx == 3:
            r = _sigmoid(r + bm_ref[:, dst:dst + PROJ_CHUNK])
        outs[idx][:, dst:dst + PROJ_CHUNK] = r.astype(BF16)

    chunks = _proj_chunks()
    for chunk in [c for c in chunks if c[1] == 0]:
        emit(chunk)
    row = lax.broadcasted_iota(jnp.int32, (BLOCK, BLOCK), 0)
    col = lax.broadcasted_iota(jnp.int32, (BLOCK, BLOCK), 1)
    causal = col <= row
    w = [jnp.where(causal, ws_ref[g], 0.0).astype(BF16) for g in range(GM_GROUPS)]
    for c in range(ROWS_PROJ // BLOCK):
        rows = slice(c * BLOCK, (c + 1) * BLOCK)
        v = pa_ref[rows, GM_WIDTH:2 * GM_WIDTH]
        mu = jnp.mean(v, axis=-1, keepdims=True)
        d = v - mu
        var = jnp.mean(d * d, axis=-1, keepdims=True)
        vn = (d * lax.rsqrt(var + EPS) * gv_ref[...]).astype(BF16)
        for g in range(GM_GROUPS):
            cols = slice(g * BLOCK, (g + 1) * BLOCK)
            mixed = jnp.dot(w[g], vn[:, cols], preferred_element_type=F32) + bs_ref[:, g:g + 1]
            u = pa_ref[rows, cols]
            gate = pa_ref[rows, 2 * GM_WIDTH + g * BLOCK:2 * GM_WIDTH + (g + 1) * BLOCK]
            ya_ref[rows, cols] = (u * mixed * _silu(gate)).astype(BF16)
    for chunk in [c for c in chunks if c[1] != 0]:
        emit(chunk)


def _front(x, g_pre, w_in, w_s, b_s_t, g_v, b_merge, cos_t, sin_t, seq):
    m = x.shape[0]
    tm = ROWS_PROJ
    nj = seq // tm
    return pl.pallas_call(
        _front_kernel,
        grid=(m // tm,),
        in_specs=[
            pl.BlockSpec((tm, D_MODEL), lambda i: (i, 0)),
            pl.BlockSpec((1, D_MODEL), lambda i: (0, 0)),
            pl.BlockSpec((D_MODEL, IN_WIDTH), lambda i: (0, 0), pipeline_mode=pl.Buffered(1)),
            pl.BlockSpec((GM_GROUPS, BLOCK, BLOCK), lambda i: (0, 0, 0)),
            pl.BlockSpec((BLOCK, GM_GROUPS), lambda i: (0, 0)),
            pl.BlockSpec((1, GM_WIDTH), lambda i: (0, 0)),
            pl.BlockSpec((1, PG_WIDTH), lambda i: (0, 0)),
            pl.BlockSpec((tm, LANES), lambda i: (i % nj, 0)),
            pl.BlockSpec((tm, LANES), lambda i: (i % nj, 0)),
        ],
        out_specs=[
            pl.BlockSpec((tm, GM_WIDTH), lambda i: (i, 0)),
            pl.BlockSpec((tm, PB_WIDTH), lambda i: (i, 0)),
            pl.BlockSpec((tm, PC_WIDTH), lambda i: (i, 0)),
            pl.BlockSpec((tm, PG_WIDTH), lambda i: (i, 0)),
        ],
        out_shape=[
            jax.ShapeDtypeStruct((m, GM_WIDTH), BF16),
            jax.ShapeDtypeStruct((m, PB_WIDTH), BF16),
            jax.ShapeDtypeStruct((m, PC_WIDTH), BF16),
            jax.ShapeDtypeStruct((m, PG_WIDTH), BF16),
        ],
        scratch_shapes=[pltpu.VMEM((tm, PA_WIDTH), F32)],
        compiler_params=pltpu.CompilerParams(
            dimension_semantics=("arbitrary",), vmem_limit_bytes=VMEM_LIMIT),
        name="front",
    )(x, g_pre, w_in, w_s, b_s_t, g_v, b_merge, cos_t, sin_t)


def _swa_kernel(sinks_ref, q_ref, gate_ref, k_ref, v_ref, o_ref,
                kprev_ref, vprev_ref, sc_ref):
    j = pl.program_id(1)

    @pl.when(j == 0)
    def _():
        kprev_ref[...] = jnp.zeros(kprev_ref.shape, BF16)
        vprev_ref[...] = jnp.zeros(vprev_ref.shape, BF16)

    lane = lax.broadcasted_iota(jnp.int32, (BLOCK, LANES), 1)
    qrow = lax.broadcasted_iota(jnp.int32, (BLOCK, LANES), 0)
    lo = lane < HEAD_DIM
    from_prev = lane > qrow
    zero = jnp.zeros((BLOCK, LANES), F32)
    no_prev = jnp.where(j == 0, NEG, 0.0)
    lo_ones = jnp.where(lo, 1.0, 0.0).astype(BF16)
    hi_ones = jnp.where(lo, 0.0, 1.0).astype(BF16)
    ones2 = jnp.concatenate([lo_ones, lo_ones, hi_ones, hi_ones], axis=0)

    def placed(x):
        xr = pltpu.roll(x, HEAD_DIM, 1)
        return [jnp.where(lo, x, zero).astype(BF16), jnp.where(lo, zero, xr).astype(BF16),
                jnp.where(lo, xr, zero).astype(BF16), jnp.where(lo, zero, x).astype(BF16)]

    kp = [kprev_ref[n] for n in range(2 * SW_KV_HEADS)]
    vp = [vprev_ref[n] for n in range(2 * SW_KV_HEADS)]
    nblk = ROWS_SWA // BLOCK
    v2s = {}
    for b in range(nblk):
        rows = slice(b * BLOCK, (b + 1) * BLOCK)
        kc = placed(k_ref[rows, :].astype(F32))
        vc = placed(v_ref[rows, :].astype(F32))
        for g in range(SW_KV_HEADS):
            v2 = jnp.concatenate([vp[2 * g], vc[2 * g], vp[2 * g + 1], vc[2 * g + 1]], axis=0)
            v2s[(b, g)] = jnp.concatenate([v2, ones2], axis=1)
        for pr in range(SW_HEADS // 2):
            g = pr // 2
            cols = slice(pr * LANES, (pr + 1) * LANES)
            k2 = jnp.concatenate([kp[2 * g], kc[2 * g], kp[2 * g + 1], kc[2 * g + 1]], axis=0)
            s = lax.dot_general(q_ref[rows, cols], k2, (((1,), (1,)), ((), ())),
                                preferred_element_type=F32)
            for hh in range(2):
                s_prev = s[:, 2 * hh * BLOCK:(2 * hh + 1) * BLOCK]
                s_cur = s[:, (2 * hh + 1) * BLOCK:(2 * hh + 2) * BLOCK]
                if b == 0:
                    s_prev = s_prev + no_prev
                sc_ref[b, pr, :, hh * BLOCK:(hh + 1) * BLOCK] = jnp.where(from_prev, s_prev, s_cur)
        kp, vp = kc, vc
    for n in range(2 * SW_KV_HEADS):
        kprev_ref[n] = kp[n]
        vprev_ref[n] = vp[n]
    for b in range(nblk):
        rows = slice(b * BLOCK, (b + 1) * BLOCK)
        for pr in range(SW_HEADS // 2):
            g = pr // 2
            cols = slice(pr * LANES, (pr + 1) * LANES)
            parts = []
            tops = []
            for hh in range(2):
                sc = sc_ref[b, pr, :, hh * BLOCK:(hh + 1) * BLOCK]
                m = jnp.max(sc, axis=-1, keepdims=True)
                p = jnp.exp(sc - m)
                tops.append(m)
                parts.append(jnp.where(from_prev, p, zero).astype(BF16))
                parts.append(jnp.where(from_prev, zero, p).astype(BF16))
            r = jnp.dot(jnp.concatenate(parts, axis=1), v2s[(b, g)],
                        preferred_element_type=F32)
            sink = jnp.where(lo[0:1, :], sinks_ref[2 * pr], sinks_ref[2 * pr + 1])
            denom = r[:, LANES:] + jnp.exp(sink - jnp.where(lo, tops[0], tops[1]))
            o = r[:, :LANES] / denom
            o_ref[rows, cols] = (o * _silu(gate_ref[rows, cols].astype(F32))).astype(BF16)


def _swa(pb, sinks, batch, seq):
    m = pb.shape[0]
    tq = ROWS_SWA
    nj = seq // tq
    kcol = 2 * SW_WIDTH // SW_KV_WIDTH
    return pl.pallas_call(
        _swa_kernel,
        grid=(batch, nj),
        in_specs=[
            pl.BlockSpec(memory_space=pltpu.SMEM),
            pl.BlockSpec((tq, SW_WIDTH), lambda b, j: (b * nj + j, 0)),
            pl.BlockSpec((tq, SW_WIDTH), lambda b, j: (b * nj + j, 1)),
            pl.BlockSpec((tq, SW_KV_WIDTH), lambda b, j: (b * nj + j, kcol)),
            pl.BlockSpec((tq, SW_KV_WIDTH), lambda b, j: (b * nj + j, kcol + 1)),
        ],
        out_specs=pl.BlockSpec((tq, SW_WIDTH), lambda b, j: (b * nj + j, 0)),
        out_shape=jax.ShapeDtypeStruct((m, SW_WIDTH), BF16),
        scratch_shapes=[
            pltpu.VMEM((2 * SW_KV_HEADS, BLOCK, LANES), BF16),
            pltpu.VMEM((2 * SW_KV_HEADS, BLOCK, LANES), BF16),
            pltpu.VMEM((ROWS_SWA // BLOCK, SW_HEADS // 2, BLOCK, 2 * BLOCK), F32),
        ],
        compiler_params=pltpu.CompilerParams(
            dimension_semantics=("arbitrary", "arbitrary"), vmem_limit_bytes=VMEM_LIMIT),
        name="swa",
    )(sinks, pb, pb, pb, pb)


def _sb_kernel(q_ref, k_ref, v_ref, gate_ref, o_ref, suf_ref, acc_ref, carry_ref,
               z_ref, hl_ref, a_ref):
    c = pl.program_id(2)
    row = lax.broadcasted_iota(jnp.int32, (BLOCK, BLOCK), 0)
    col = lax.broadcasted_iota(jnp.int32, (BLOCK, BLOCK), 1)
    before = col < row
    one_suffix = jnp.where(row >= col, 1.0, 0.0).astype(BF16)
    half = jnp.concatenate([one_suffix, jnp.ones((BLOCK, BLOCK), BF16)], axis=1)
    suf_ref[...] = jnp.concatenate([half, half], axis=0)
    groups = ROWS_SB // (SB_GROUP * BLOCK)
    contract_lanes = (((1,), (1,)), ((), ()))
    stop = SB_STOP * LOG2E

    def fail_of(zl):
        return jnp.maximum(zl, 0.0) + jnp.log2(1.0 + jnp.exp2(-jnp.abs(zl)))

    def split(f):
        hi = f.astype(BF16)
        lo = (f - hi.astype(F32)).astype(BF16)
        return jnp.concatenate([hi, lo], axis=1)

    def suffix_and_total(hl):
        r = jnp.dot(hl, suf_ref[...], preferred_element_type=F32)
        return r[:, :BLOCK], r[:, BLOCK:]

    def one_tile(q, m, carry, acc, diag):
        k0 = pl.multiple_of(m * BLOCK, BLOCK)
        zl = lax.dot_general(q, k_ref[pl.ds(k0, BLOCK), :], contract_lanes,
                             preferred_element_type=F32)
        f = fail_of(zl)
        if diag:
            f = jnp.where(before, f, 0.0)
        cs, tot = suffix_and_total(split(f))
        a = jnp.exp2(zl - cs - carry)
        if diag:
            a = jnp.where(before, a, 0.0)
        acc = acc + jnp.dot(a.astype(BF16), v_ref[pl.ds(k0, BLOCK), :],
                            preferred_element_type=F32)
        return carry + tot, acc

    def rows_of(blk):
        return pl.ds(pl.multiple_of(blk * BLOCK, BLOCK), BLOCK)

    def finish(blk, acc):
        rows = rows_of(blk)
        o_ref[rows, :] = (acc * _silu(gate_ref[rows, :].astype(F32))).astype(BF16)

    def walk_block(blk, start_offset):
        q = q_ref[rows_of(blk), :]

        def cond(st):
            m, least, _, _ = st
            return jnp.logical_and(m >= 0, least < stop)

        def body(st):
            m, _, carry, acc = st
            carry, acc = one_tile(q, m, carry, acc, False)
            return m - 1, jnp.min(carry), carry, acc

        carry = carry_ref[blk]
        start = c * (ROWS_SB // BLOCK) + blk - start_offset
        _, _, _, acc = lax.while_loop(cond, body, (start, jnp.min(carry), carry, acc_ref[blk]))
        finish(blk, acc)

    def walk_group(gl):
        def walk(ib, _):
            walk_block(gl * SB_GROUP + ib, SB_FAST_TILES)
            return 0
        lax.fori_loop(0, SB_GROUP, walk, 0)

    def group(gl, least_prev):
        g0 = c * (ROWS_SB // BLOCK) + gl * SB_GROUP
        span = SB_FAST_TILES * BLOCK
        diag = SB_FAST_TILES - 1
        spans = []
        for ib in range(SB_GROUP):
            blk = gl * SB_GROUP + ib
            k0 = pl.multiple_of(jnp.maximum(g0 + ib - diag, 0) * BLOCK, BLOCK)
            spans.append(pl.ds(k0, span))
            zl = lax.dot_general(q_ref[rows_of(blk), :], k_ref[spans[ib], :], contract_lanes,
                                 preferred_element_type=F32)
            f = fail_of(zl)
            z_ref[ib] = zl
            for t in range(SB_FAST_TILES):
                cols = slice(t * BLOCK, (t + 1) * BLOCK)
                hl_ref[ib, t] = split(jnp.where(before, f[:, cols], 0.0) if t == diag
                                      else f[:, cols])
        least = None
        for ib in range(SB_GROUP):
            carry = None
            for t in reversed(range(SB_FAST_TILES)):
                cols = slice(t * BLOCK, (t + 1) * BLOCK)
                cs, tot = suffix_and_total(hl_ref[ib, t])
                e = z_ref[ib, :, cols] - cs
                a = jnp.exp2(e if carry is None else e - carry)
                if t == diag:
                    a = jnp.where(before, a, 0.0)
                a_ref[ib, :, cols] = a.astype(BF16)
                carry = tot if carry is None else carry + tot
            carry_ref[gl * SB_GROUP + ib] = carry
            least = carry if least is None else jnp.minimum(least, carry)
        for ib in range(SB_GROUP):
            blk = gl * SB_GROUP + ib
            acc = jnp.dot(a_ref[ib], v_ref[spans[ib], :], preferred_element_type=F32)
            finish(blk, acc)
            acc_ref[blk] = acc

        @pl.when(jnp.min(least_prev) < stop)
        def _():
            walk_group(gl - 1)

        return least

    settled = jnp.full((BLOCK, BLOCK), 2.0 * stop, F32)
    least_last = lax.fori_loop(0, groups, group, settled)

    @pl.when(jnp.min(least_last) < stop)
    def _():
        walk_group(groups - 1)

    @pl.when(c == 0)
    def _():
        def redo(ib, _):
            carry, acc = one_tile(q_ref[rows_of(ib), :], ib, jnp.zeros((BLOCK, BLOCK), F32),
                                  jnp.zeros((BLOCK, SB_HEAD_DIM), F32), True)
            carry_ref[ib] = carry
            acc_ref[ib] = acc
            walk_block(ib, 1)
            return 0
        lax.fori_loop(0, SB_FAST_TILES - 1, redo, 0)


def _sb(pc, batch, seq):
    m = pc.shape[0]
    tq = ROWS_SB
    nc = seq // tq
    hb = SB_WIDTH // SB_HEAD_DIM
    return pl.pallas_call(
        _sb_kernel,
        grid=(batch, SB_HEADS, nc),
        in_specs=[
            pl.BlockSpec((tq, SB_HEAD_DIM), lambda b, h, c: (b * nc + c, h)),
            pl.BlockSpec((seq, SB_HEAD_DIM), lambda b, h, c: (b, hb + h)),
            pl.BlockSpec((seq, SB_HEAD_DIM), lambda b, h, c: (b, 2 * hb + h)),
            pl.BlockSpec((tq, SB_HEAD_DIM), lambda b, h, c: (b * nc + c, 3 * hb + h)),
        ],
        out_specs=pl.BlockSpec((tq, SB_HEAD_DIM), lambda b, h, c: (b * nc + c, h)),
        out_shape=jax.ShapeDtypeStruct((m, SB_WIDTH), BF16),
        scratch_shapes=[
            pltpu.VMEM((2 * BLOCK, 2 * BLOCK), BF16),
            pltpu.VMEM((ROWS_SB // BLOCK, BLOCK, SB_HEAD_DIM), F32),
            pltpu.VMEM((ROWS_SB // BLOCK, BLOCK, BLOCK), F32),
            pltpu.VMEM((SB_GROUP, BLOCK, SB_FAST_TILES * BLOCK), F32),
            pltpu.VMEM((SB_GROUP, SB_FAST_TILES, BLOCK, 2 * BLOCK), BF16),
            pltpu.VMEM((SB_GROUP, BLOCK, SB_FAST_TILES * BLOCK), BF16),
        ],
        compiler_params=pltpu.CompilerParams(
            dimension_semantics=("arbitrary", "arbitrary", "arbitrary"),
            vmem_limit_bytes=VMEM_LIMIT),
        name="sb",
    )(pc, pc, pc, pc)


def _merge_kernel(ya_ref, yb_ref, yc_ref, pg_ref, x_ref, wa_ref, wb_ref, wc_ref,
                  wo_ref, gp_ref, o_ref):
    merged = None
    for n, (y_ref, w_ref) in enumerate(((ya_ref, wa_ref), (yb_ref, wb_ref), (yc_ref, wc_ref))):
        cols = slice(n * D_MODEL, (n + 1) * D_MODEL)
        term = pg_ref[:, cols].astype(F32) * jnp.dot(y_ref[...], w_ref[...],
                                                     preferred_element_type=F32)
        merged = term if merged is None else merged + term
    out = jnp.dot(merged.astype(BF16), wo_ref[...], preferred_element_type=F32)
    ms = jnp.mean(out * out, axis=-1, keepdims=True)
    o_ref[...] = x_ref[...] + out * lax.rsqrt(ms + EPS) * gp_ref[...]


def _merge(ya, yb, yc, pg, x, wa, wb, wc, w_out, g_post):
    m = x.shape[0]
    tm = ROWS_MERGE
    row = lambda width: pl.BlockSpec((tm, width), lambda i: (i, 0))
    full = lambda a: pl.BlockSpec(a.shape, lambda i: (0, 0))
    return pl.pallas_call(
        _merge_kernel,
        grid=(m // tm,),
        in_specs=[row(GM_WIDTH), row(SW_WIDTH), row(SB_WIDTH), row(PG_WIDTH), row(D_MODEL),
                  full(wa), full(wb), full(wc), full(w_out), full(g_post)],
        out_specs=row(D_MODEL),
        out_shape=jax.ShapeDtypeStruct((m, D_MODEL), F32),
        compiler_params=pltpu.CompilerParams(
            dimension_semantics=("arbitrary",), vmem_limit_bytes=VMEM_LIMIT),
        name="merge",
    )(ya, yb, yc, pg, x, wa, wb, wc, w_out, g_post)


def _rope_tables(seq):
    half = HEAD_DIM // 2
    freqs = ROPE_THETA ** (-jnp.arange(half, dtype=F32) / half)
    ang = jnp.arange(seq).astype(F32)[:, None] * freqs[None, :]
    cos = jnp.tile(jnp.cos(ang), (1, LANES // half))
    sin = jnp.tile(jnp.concatenate([-jnp.sin(ang), jnp.sin(ang)], axis=-1), (1, LANES // HEAD_DIM))
    return cos, sin


def kernel(x, w_in, gm_w_s, gm_b_s, gm_norm_gain, sw_sinks, w_branch_a, w_branch_b, w_branch_c,
           b_merge, w_out, g_pre, g_post):
    batch, seq, _ = x.shape
    assert seq % ROWS_SB == 0 and (batch * seq) % ROWS_PROJ == 0
    depth = w_in.shape[0]
    cos_t, sin_t = _rope_tables(seq)
    xf = x.reshape(batch * seq, D_MODEL)
    for l in range(depth):
        ya, pb, pc, pg = _front(xf, g_pre[l][None, :], w_in[l].astype(BF16), gm_w_s[l],
                                gm_b_s[l].T, gm_norm_gain[l][None, :],
                                b_merge[l].reshape(1, PG_WIDTH), cos_t, sin_t, seq)
        yb = _swa(pb, sw_sinks[l], batch, seq)
        yc = _sb(pc, batch, seq)
        xf = _merge(ya, yb, yc, pg, xf, w_branch_a[l].astype(BF16), w_branch_b[l].astype(BF16),
                    w_branch_c[l].astype(BF16), w_out[l].astype(BF16), g_post[l][None, :])
    return xf.reshape(batch, seq, D_MODEL)
```

```python
import jax
import jax.numpy as jnp
from jax import lax
from jax.experimental import pallas as pl
from jax.experimental.pallas import tpu as pltpu

F32 = jnp.float32
BF16 = jnp.bfloat16

D_MODEL = 1024
BLOCK = 128
EPS = 1e-6
NEG = -1e30
GM_GROUPS = 4
GM_WIDTH = 512
HEAD_DIM = 64
SW_HEADS = 8
SW_KV_HEADS = 2
SW_WIDTH = 512
SW_KV_WIDTH = 128
ROPE_THETA = 10000.0
SB_HEADS = 4
SB_HEAD_DIM = 128
SB_WIDTH = 512
N_BRANCH = 3
IN_WIDTH = 7936

V7X_VMEM_BYTES = 64 * 1024 * 1024
LANES = 128
VMEM_LIMIT = V7X_VMEM_BYTES - 8 * 1024 * 1024

ROWS_PROJ = 512
ROWS_SWA = 512
ROWS_SB = 2048
ROWS_MERGE = 512
SB_GROUP = 8
SB_TOP_ROWS = 64

PA_WIDTH = 3 * GM_WIDTH
PB_WIDTH = 2 * SW_WIDTH + 2 * SW_KV_WIDTH
PC_WIDTH = 4 * SB_WIDTH
PG_WIDTH = N_BRANCH * D_MODEL
PROJ_CHUNK = 256

SB_EXP_ZERO = 104.0
SB_STOP = SB_EXP_ZERO * 1.1
LOG2E = 1.4426950408889634


def _proj_chunks():
    segs = [
        (0, 1536, 0, 0, 1.0),
        (1536, 2048, 1, 0, HEAD_DIM ** -0.5),
        (2048, 2304, 1, 2 * SW_WIDTH, 1.0),
        (2304, 2816, 1, SW_WIDTH, 1.0),
        (2816, 3328, 2, 0, SB_HEAD_DIM ** -0.5 * LOG2E),
        (3328, 4864, 2, SB_WIDTH, 1.0),
        (4864, 7936, 3, 0, 1.0),
    ]
    out = []
    for lo, hi, idx, dst, scale in segs:
        for c in range(lo, hi, PROJ_CHUNK):
            out.append((c, idx, dst + c - lo, scale))
    return out


def _sigmoid(x):
    return 1.0 / (1.0 + jnp.exp(-x))


def _silu(x):
    return x * _sigmoid(x)


def _rope(x, cos, sin):
    lane = lax.broadcasted_iota(jnp.int32, x.shape, 1)
    rot_fwd = (lane % HEAD_DIM) < (HEAD_DIM // 2)
    rot = jnp.where(rot_fwd, pltpu.roll(x, LANES - HEAD_DIM // 2, 1),
                    pltpu.roll(x, HEAD_DIM // 2, 1))
    return x * cos + rot * sin


def _front_kernel(x_ref, g_ref, w_ref, ws_ref, bs_ref, gv_ref, bm_ref, cos_ref, sin_ref,
                  ya_ref, pb_ref, pc_ref, pg_ref, pa_ref):
    x = x_ref[...]
    ms = jnp.mean(x * x, axis=-1, keepdims=True)
    h = (x * lax.rsqrt(ms + EPS) * g_ref[...]).astype(BF16)
    outs = (pa_ref, pb_ref, pc_ref, pg_ref)
    cos = cos_ref[...]
    sin = sin_ref[...]

    def emit(chunk):
        src, idx, dst, scale = chunk
        r = jnp.dot(h, w_ref[:, src:src + PROJ_CHUNK], preferred_element_type=F32)
        if scale != 1.0:
            r = r * scale
        if idx == 0:
            pa_ref[:, dst:dst + PROJ_CHUNK] = r
            return
        if idx == 1 and dst < SW_WIDTH:
            r = jnp.concatenate([_rope(r[:, :LANES], cos, sin), _rope(r[:, LANES:], cos, sin)],
                                axis=1)
        elif idx == 1 and dst == 2 * SW_WIDTH:
            r = jnp.concatenate([_rope(r[:, :LANES], cos, sin), r[:, LANES:]], axis=1)
        elif idx == 3:
            r = _sigmoid(r + bm_ref[:, dst:dst + PROJ_CHUNK])
        outs[idx][:, dst:dst + PROJ_CHUNK] = r.astype(BF16)

    chunks = _proj_chunks()
    for chunk in [c for c in chunks if c[1] == 0]:
        emit(chunk)
    row = lax.broadcasted_iota(jnp.int32, (BLOCK, BLOCK), 0)
    col = lax.broadcasted_iota(jnp.int32, (BLOCK, BLOCK), 1)
    causal = col <= row
    w = [jnp.where(causal, ws_ref[g], 0.0).astype(BF16) for g in range(GM_GROUPS)]
    for c in range(ROWS_PROJ // BLOCK):
        rows = slice(c * BLOCK, (c + 1) * BLOCK)
        v = pa_ref[rows, GM_WIDTH:2 * GM_WIDTH]
        mu = jnp.mean(v, axis=-1, keepdims=True)
        d = v - mu
        var = jnp.mean(d * d, axis=-1, keepdims=True)
        vn = (d * lax.rsqrt(var + EPS) * gv_ref[...]).astype(BF16)
        for g in range(GM_GROUPS):
            cols = slice(g * BLOCK, (g + 1) * BLOCK)
            mixed = jnp.dot(w[g], vn[:, cols], preferred_element_type=F32) + bs_ref[:, g:g + 1]
            u = pa_ref[rows, cols]
            gate = pa_ref[rows, 2 * GM_WIDTH + g * BLOCK:2 * GM_WIDTH + (g + 1) * BLOCK]
            ya_ref[rows, cols] = (u * mixed * _silu(gate)).astype(BF16)
    for chunk in [c for c in chunks if c[1] != 0]:
        emit(chunk)


def _front(x, g_pre, w_in, layer, w_s, b_s_t, g_v, b_merge, cos_t, sin_t, seq):
    m = x.shape[0]
    tm = ROWS_PROJ
    nj = seq // tm
    return pl.pallas_call(
        _front_kernel,
        grid=(m // tm,),
        in_specs=[
            pl.BlockSpec((tm, D_MODEL), lambda i: (i, 0)),
            pl.BlockSpec((1, D_MODEL), lambda i: (0, 0)),
            pl.BlockSpec((None, D_MODEL, IN_WIDTH), lambda i: (layer, 0, 0),
                         pipeline_mode=pl.Buffered(1)),
            pl.BlockSpec((GM_GROUPS, BLOCK, BLOCK), lambda i: (0, 0, 0)),
            pl.BlockSpec((BLOCK, GM_GROUPS), lambda i: (0, 0)),
            pl.BlockSpec((1, GM_WIDTH), lambda i: (0, 0)),
            pl.BlockSpec((1, PG_WIDTH), lambda i: (0, 0)),
            pl.BlockSpec((tm, LANES), lambda i: (i % nj, 0)),
            pl.BlockSpec((tm, LANES), lambda i: (i % nj, 0)),
        ],
        out_specs=[
            pl.BlockSpec((tm, GM_WIDTH), lambda i: (i, 0)),
            pl.BlockSpec((tm, PB_WIDTH), lambda i: (i, 0)),
            pl.BlockSpec((tm, PC_WIDTH), lambda i: (i, 0)),
            pl.BlockSpec((tm, PG_WIDTH), lambda i: (i, 0)),
        ],
        out_shape=[
            jax.ShapeDtypeStruct((m, GM_WIDTH), BF16),
            jax.ShapeDtypeStruct((m, PB_WIDTH), BF16),
            jax.ShapeDtypeStruct((m, PC_WIDTH), BF16),
            jax.ShapeDtypeStruct((m, PG_WIDTH), BF16),
        ],
        scratch_shapes=[pltpu.VMEM((tm, PA_WIDTH), F32)],
        compiler_params=pltpu.CompilerParams(
            dimension_semantics=("arbitrary",), vmem_limit_bytes=VMEM_LIMIT),
        name="front",
    )(x, g_pre, w_in, w_s, b_s_t, g_v, b_merge, cos_t, sin_t)


def _swa_kernel(sinks_ref, q_ref, gate_ref, k_ref, v_ref, o_ref,
                kprev_ref, vprev_ref, sc_ref):
    j = pl.program_id(1)

    @pl.when(j == 0)
    def _():
        kprev_ref[...] = jnp.zeros(kprev_ref.shape, BF16)
        vprev_ref[...] = jnp.zeros(vprev_ref.shape, BF16)

    lane = lax.broadcasted_iota(jnp.int32, (BLOCK, LANES), 1)
    qrow = lax.broadcasted_iota(jnp.int32, (BLOCK, LANES), 0)
    lo = lane < HEAD_DIM
    from_prev = lane > qrow
    zero = jnp.zeros((BLOCK, LANES), F32)
    no_prev = jnp.where(j == 0, NEG, 0.0)
    lo_ones = jnp.where(lo, 1.0, 0.0).astype(BF16)
    hi_ones = jnp.where(lo, 0.0, 1.0).astype(BF16)
    ones2 = jnp.concatenate([lo_ones, lo_ones, hi_ones, hi_ones], axis=0)

    def placed(x):
        xr = pltpu.roll(x, HEAD_DIM, 1)
        return [jnp.where(lo, x, zero).astype(BF16), jnp.where(lo, zero, xr).astype(BF16),
                jnp.where(lo, xr, zero).astype(BF16), jnp.where(lo, zero, x).astype(BF16)]

    kp = [kprev_ref[n] for n in range(2 * SW_KV_HEADS)]
    vp = [vprev_ref[n] for n in range(2 * SW_KV_HEADS)]
    nblk = ROWS_SWA // BLOCK
    v2s = {}
    for b in range(nblk):
        rows = slice(b * BLOCK, (b + 1) * BLOCK)
        kc = placed(k_ref[rows, :].astype(F32))
        vc = placed(v_ref[rows, :].astype(F32))
        for g in range(SW_KV_HEADS):
            v2 = jnp.concatenate([vp[2 * g], vc[2 * g], vp[2 * g + 1], vc[2 * g + 1]], axis=0)
            v2s[(b, g)] = jnp.concatenate([v2, ones2], axis=1)
        for pr in range(SW_HEADS // 2):
            g = pr // 2
            cols = slice(pr * LANES, (pr + 1) * LANES)
            k2 = jnp.concatenate([kp[2 * g], kc[2 * g], kp[2 * g + 1], kc[2 * g + 1]], axis=0)
            s = lax.dot_general(q_ref[rows, cols], k2, (((1,), (1,)), ((), ())),
                                preferred_element_type=F32)
            for hh in range(2):
                s_prev = s[:, 2 * hh * BLOCK:(2 * hh + 1) * BLOCK]
                s_cur = s[:, (2 * hh + 1) * BLOCK:(2 * hh + 2) * BLOCK]
                if b == 0:
                    s_prev = s_prev + no_prev
                sc_ref[b, pr, :, hh * BLOCK:(hh + 1) * BLOCK] = jnp.where(from_prev, s_prev, s_cur)
        kp, vp = kc, vc
    for n in range(2 * SW_KV_HEADS):
        kprev_ref[n] = kp[n]
        vprev_ref[n] = vp[n]
    for b in range(nblk):
        rows = slice(b * BLOCK, (b + 1) * BLOCK)
        for pr in range(SW_HEADS // 2):
            g = pr // 2
            cols = slice(pr * LANES, (pr + 1) * LANES)
            parts = []
            tops = []
            for hh in range(2):
                sc = sc_ref[b, pr, :, hh * BLOCK:(hh + 1) * BLOCK]
                m = jnp.max(sc, axis=-1, keepdims=True)
                p = jnp.exp(sc - m)
                tops.append(m)
                parts.append(jnp.where(from_prev, p, zero).astype(BF16))
                parts.append(jnp.where(from_prev, zero, p).astype(BF16))
            r = jnp.dot(jnp.concatenate(parts, axis=1), v2s[(b, g)],
                        preferred_element_type=F32)
            sink = jnp.where(lo[0:1, :], sinks_ref[2 * pr], sinks_ref[2 * pr + 1])
            denom = r[:, LANES:] + jnp.exp(sink - jnp.where(lo, tops[0], tops[1]))
            o = r[:, :LANES] / denom
            o_ref[rows, cols] = (o * _silu(gate_ref[rows, cols].astype(F32))).astype(BF16)


def _swa(pb, sinks, batch, seq):
    m = pb.shape[0]
    tq = ROWS_SWA
    nj = seq // tq
    kcol = 2 * SW_WIDTH // SW_KV_WIDTH
    return pl.pallas_call(
        _swa_kernel,
        grid=(batch, nj),
        in_specs=[
            pl.BlockSpec(memory_space=pltpu.SMEM),
            pl.BlockSpec((tq, SW_WIDTH), lambda b, j: (b * nj + j, 0)),
            pl.BlockSpec((tq, SW_WIDTH), lambda b, j: (b * nj + j, 1)),
            pl.BlockSpec((tq, SW_KV_WIDTH), lambda b, j: (b * nj + j, kcol)),
            pl.BlockSpec((tq, SW_KV_WIDTH), lambda b, j: (b * nj + j, kcol + 1)),
        ],
        out_specs=pl.BlockSpec((tq, SW_WIDTH), lambda b, j: (b * nj + j, 0)),
        out_shape=jax.ShapeDtypeStruct((m, SW_WIDTH), BF16),
        scratch_shapes=[
            pltpu.VMEM((2 * SW_KV_HEADS, BLOCK, LANES), BF16),
            pltpu.VMEM((2 * SW_KV_HEADS, BLOCK, LANES), BF16),
            pltpu.VMEM((ROWS_SWA // BLOCK, SW_HEADS // 2, BLOCK, 2 * BLOCK), F32),
        ],
        compiler_params=pltpu.CompilerParams(
            dimension_semantics=("arbitrary", "arbitrary"), vmem_limit_bytes=VMEM_LIMIT),
        name="swa",
    )(sinks, pb, pb, pb, pb)


def _sb_kernel(q_ref, k_ref, v_ref, gate_ref, o_ref, suf_ref, acc_ref, carry_ref,
               z_ref, hl_ref, a_ref, zt_ref, hlt_ref, at_ref):
    c = pl.program_id(2)
    row = lax.broadcasted_iota(jnp.int32, (BLOCK, BLOCK), 0)
    col = lax.broadcasted_iota(jnp.int32, (BLOCK, BLOCK), 1)
    before = col < row
    one_suffix = jnp.where(row >= col, 1.0, 0.0).astype(BF16)
    half = jnp.concatenate([one_suffix, jnp.ones((BLOCK, BLOCK), BF16)], axis=1)
    suf_ref[...] = jnp.concatenate([half, half], axis=0)
    groups = ROWS_SB // (SB_GROUP * BLOCK)
    contract_lanes = (((1,), (1,)), ((), ()))
    stop = SB_STOP * LOG2E

    def fail_of(zl):
        return jnp.maximum(zl, 0.0) + jnp.log2(1.0 + jnp.exp2(-jnp.abs(zl)))

    def split(f):
        hi = f.astype(BF16)
        lo = (f - hi.astype(F32)).astype(BF16)
        return jnp.concatenate([hi, lo], axis=1)

    def suffix_and_total(hl):
        r = jnp.dot(hl, suf_ref[...], preferred_element_type=F32)
        return r[:, :BLOCK], r[:, BLOCK:]

    def one_tile(q, m, carry, acc, diag):
        k0 = pl.multiple_of(m * BLOCK, BLOCK)
        zl = lax.dot_general(q, k_ref[pl.ds(k0, BLOCK), :], contract_lanes,
                             preferred_element_type=F32)
        f = fail_of(zl)
        if diag:
            f = jnp.where(before, f, 0.0)
        cs, tot = suffix_and_total(split(f))
        a = jnp.exp2(zl - cs - carry)
        if diag:
            a = jnp.where(before, a, 0.0)
        acc = acc + jnp.dot(a.astype(BF16), v_ref[pl.ds(k0, BLOCK), :],
                            preferred_element_type=F32)
        return carry + tot, acc

    def rows_of(blk):
        return pl.ds(pl.multiple_of(blk * BLOCK, BLOCK), BLOCK)

    def finish(blk, acc):
        rows = rows_of(blk)
        o_ref[rows, :] = (acc * _silu(gate_ref[rows, :].astype(F32))).astype(BF16)

    def walk_block(blk, start_offset):
        q = q_ref[rows_of(blk), :]

        def cond(st):
            m, least, _, _ = st
            return jnp.logical_and(m >= 0, least < stop)

        def body(st):
            m, _, carry, acc = st
            carry, acc = one_tile(q, m, carry, acc, False)
            return m - 1, jnp.min(carry), carry, acc

        carry = carry_ref[blk]
        start = c * (ROWS_SB // BLOCK) + blk - start_offset
        _, _, _, acc = lax.while_loop(cond, body, (start, jnp.min(carry), carry, acc_ref[blk]))
        finish(blk, acc)

    def walk_group(gl):
        def walk(ib, _):
            walk_block(gl * SB_GROUP + ib, 2)
            return 0
        lax.fori_loop(0, SB_GROUP, walk, 0)

    def group(gl, least_prev):
        g0 = c * (ROWS_SB // BLOCK) + gl * SB_GROUP
        top = SB_TOP_ROWS
        spans, tops = [], []
        for ib in range(SB_GROUP):
            blk = gl * SB_GROUP + ib
            k1 = pl.multiple_of(jnp.maximum(g0 + ib - 1, 0) * BLOCK, BLOCK)
            k2 = pl.multiple_of(jnp.maximum(g0 + ib - 2, 0) * BLOCK, BLOCK)
            spans.append(pl.ds(k1, 2 * BLOCK))
            tops.append(pl.ds(k2, BLOCK))
            q = q_ref[rows_of(blk), :]
            zl = lax.dot_general(q, k_ref[spans[ib], :], contract_lanes,
                                 preferred_element_type=F32)
            zt = lax.dot_general(q[:top], k_ref[tops[ib], :], contract_lanes,
                                 preferred_element_type=F32)
            f = fail_of(zl)
            z_ref[ib] = zl
            zt_ref[ib] = zt
            hl_ref[ib, 0] = split(f[:, :BLOCK])
            hl_ref[ib, 1] = split(jnp.where(before, f[:, BLOCK:], 0.0))
            hlt_ref[ib] = split(fail_of(zt))
        least = None
        for ib in range(SB_GROUP):
            cs, carry = suffix_and_total(hl_ref[ib, 1])
            a_ref[ib, :, BLOCK:] = jnp.where(
                before, jnp.exp2(z_ref[ib, :, BLOCK:] - cs), 0.0).astype(BF16)
            cs, tot = suffix_and_total(hl_ref[ib, 0])
            a_ref[ib, :, :BLOCK] = jnp.exp2(z_ref[ib, :, :BLOCK] - cs - carry).astype(BF16)
            carry = carry + tot
            cs, tot = suffix_and_total(hlt_ref[ib])
            at_ref[ib] = jnp.exp2(zt_ref[ib] - cs - carry[:top]).astype(BF16)
            carry_ref[gl * SB_GROUP + ib] = carry
            reach = jnp.concatenate([carry[:top] + tot, carry[top:]], axis=0)
            least = reach if least is None else jnp.minimum(least, reach)
        for ib in range(SB_GROUP):
            blk = gl * SB_GROUP + ib
            acc = jnp.dot(a_ref[ib], v_ref[spans[ib], :], preferred_element_type=F32)
            acc_top = jnp.dot(at_ref[ib], v_ref[tops[ib], :], preferred_element_type=F32)
            acc_ref[blk] = acc
            finish(blk, jnp.concatenate([acc[:top] + acc_top, acc[top:]], axis=0))

        @pl.when(jnp.min(least_prev) < stop)
        def _():
            walk_group(gl - 1)

        return least

    settled = jnp.full((BLOCK, BLOCK), 2.0 * stop, F32)
    least_last = lax.fori_loop(0, groups, group, settled)

    @pl.when(jnp.min(least_last) < stop)
    def _():
        walk_group(groups - 1)

    @pl.when(c == 0)
    def _():
        def redo(ib, _):
            carry, acc = one_tile(q_ref[rows_of(ib), :], ib, jnp.zeros((BLOCK, BLOCK), F32),
                                  jnp.zeros((BLOCK, SB_HEAD_DIM), F32), True)
            carry_ref[ib] = carry
            acc_ref[ib] = acc
            walk_block(ib, 1)
            return 0
        lax.fori_loop(0, 2, redo, 0)


def _sb(pc, batch, seq):
    m = pc.shape[0]
    tq = ROWS_SB
    nc = seq // tq
    hb = SB_WIDTH // SB_HEAD_DIM
    return pl.pallas_call(
        _sb_kernel,
        grid=(batch, SB_HEADS, nc),
        in_specs=[
            pl.BlockSpec((tq, SB_HEAD_DIM), lambda b, h, c: (b * nc + c, h)),
            pl.BlockSpec((seq, SB_HEAD_DIM), lambda b, h, c: (b, hb + h)),
            pl.BlockSpec((seq, SB_HEAD_DIM), lambda b, h, c: (b, 2 * hb + h)),
            pl.BlockSpec((tq, SB_HEAD_DIM), lambda b, h, c: (b * nc + c, 3 * hb + h)),
        ],
        out_specs=pl.BlockSpec((tq, SB_HEAD_DIM), lambda b, h, c: (b * nc + c, h)),
        out_shape=jax.ShapeDtypeStruct((m, SB_WIDTH), BF16),
        scratch_shapes=[
            pltpu.VMEM((2 * BLOCK, 2 * BLOCK), BF16),
            pltpu.VMEM((ROWS_SB // BLOCK, BLOCK, SB_HEAD_DIM), F32),
            pltpu.VMEM((ROWS_SB // BLOCK, BLOCK, BLOCK), F32),
            pltpu.VMEM((SB_GROUP, BLOCK, 2 * BLOCK), F32),
            pltpu.VMEM((SB_GROUP, 2, BLOCK, 2 * BLOCK), BF16),
            pltpu.VMEM((SB_GROUP, BLOCK, 2 * BLOCK), BF16),
            pltpu.VMEM((SB_GROUP, SB_TOP_ROWS, BLOCK), F32),
            pltpu.VMEM((SB_GROUP, SB_TOP_ROWS, 2 * BLOCK), BF16),
            pltpu.VMEM((SB_GROUP, SB_TOP_ROWS, BLOCK), BF16),
        ],
        compiler_params=pltpu.CompilerParams(
            dimension_semantics=("arbitrary", "arbitrary", "arbitrary"),
            vmem_limit_bytes=VMEM_LIMIT),
        name="sb",
    )(pc, pc, pc, pc)


def _merge_kernel(ya_ref, yb_ref, yc_ref, pg_ref, x_ref, wa_ref, wb_ref, wc_ref,
                  wo_ref, gp_ref, o_ref):
    merged = None
    for n, (y_ref, w_ref) in enumerate(((ya_ref, wa_ref), (yb_ref, wb_ref), (yc_ref, wc_ref))):
        cols = slice(n * D_MODEL, (n + 1) * D_MODEL)
        term = pg_ref[:, cols].astype(F32) * jnp.dot(y_ref[...], w_ref[...],
                                                     preferred_element_type=F32)
        merged = term if merged is None else merged + term
    out = jnp.dot(merged.astype(BF16), wo_ref[...], preferred_element_type=F32)
    ms = jnp.mean(out * out, axis=-1, keepdims=True)
    o_ref[...] = x_ref[...] + out * lax.rsqrt(ms + EPS) * gp_ref[...]


def _merge(ya, yb, yc, pg, x, wa, wb, wc, w_out, layer, g_post):
    m = x.shape[0]
    tm = ROWS_MERGE
    row = lambda width: pl.BlockSpec((tm, width), lambda i: (i, 0))
    full = lambda a: pl.BlockSpec(a.shape, lambda i: (0, 0))
    of_layer = lambda a: pl.BlockSpec((None,) + a.shape[1:], lambda i: (layer, 0, 0))
    return pl.pallas_call(
        _merge_kernel,
        grid=(m // tm,),
        in_specs=[row(GM_WIDTH), row(SW_WIDTH), row(SB_WIDTH), row(PG_WIDTH), row(D_MODEL),
                  of_layer(wa), of_layer(wb), of_layer(wc), of_layer(w_out), full(g_post)],
        out_specs=row(D_MODEL),
        out_shape=jax.ShapeDtypeStruct((m, D_MODEL), F32),
        compiler_params=pltpu.CompilerParams(
            dimension_semantics=("arbitrary",), vmem_limit_bytes=VMEM_LIMIT),
        name="merge",
    )(ya, yb, yc, pg, x, wa, wb, wc, w_out, g_post)


def _rope_tables(seq):
    half = HEAD_DIM // 2
    freqs = ROPE_THETA ** (-jnp.arange(half, dtype=F32) / half)
    ang = jnp.arange(seq).astype(F32)[:, None] * freqs[None, :]
    cos = jnp.tile(jnp.cos(ang), (1, LANES // half))
    sin = jnp.tile(jnp.concatenate([-jnp.sin(ang), jnp.sin(ang)], axis=-1), (1, LANES // HEAD_DIM))
    return cos, sin


def kernel(x, w_in, gm_w_s, gm_b_s, gm_norm_gain, sw_sinks, w_branch_a, w_branch_b, w_branch_c,
           b_merge, w_out, g_pre, g_post):
    batch, seq, _ = x.shape
    assert seq % ROWS_SB == 0 and (batch * seq) % ROWS_PROJ == 0
    depth = w_in.shape[0]
    cos_t, sin_t = _rope_tables(seq)
    xf = x.reshape(batch * seq, D_MODEL)
    w_in, w_out = w_in.astype(BF16), w_out.astype(BF16)
    wa, wb, wc = w_branch_a.astype(BF16), w_branch_b.astype(BF16), w_branch_c.astype(BF16)
    for l in range(depth):
        ya, pb, pc, pg = _front(xf, g_pre[l][None, :], w_in, l, gm_w_s[l], gm_b_s[l].T,
                                gm_norm_gain[l][None, :], b_merge[l].reshape(1, PG_WIDTH),
                                cos_t, sin_t, seq)
        yb = _swa(pb, sw_sinks[l], batch, seq)
        yc = _sb(pc, batch, seq)
        xf = _merge(ya, yb, yc, pg, xf, wa, wb, wc, w_out, l, g_post[l][None, :])
    return xf.reshape(batch, seq, D_MODEL)
```

```python
import jax
import jax.numpy as jnp
from jax import lax
from jax.experimental import pallas as pl
from jax.experimental.pallas import tpu as pltpu

F32 = jnp.float32
BF16 = jnp.bfloat16

D_MODEL = 1024
BLOCK = 128
EPS = 1e-6
NEG = -1e30
GM_GROUPS = 4
GM_WIDTH = 512
HEAD_DIM = 64
SW_HEADS = 8
SW_KV_HEADS = 2
SW_WIDTH = 512
SW_KV_WIDTH = 128
ROPE_THETA = 10000.0
SB_HEADS = 4
SB_HEAD_DIM = 128
SB_WIDTH = 512
N_BRANCH = 3
IN_WIDTH = 7936

V7X_VMEM_BYTES = 64 * 1024 * 1024
LANES = 128
VMEM_LIMIT = V7X_VMEM_BYTES - 8 * 1024 * 1024

ROWS_PROJ = 512
ROWS_SWA = 512
ROWS_SB = 2048
ROWS_MERGE = 512
SB_GROUP = 16
SB_TOP_ROWS = 64

PA_WIDTH = 3 * GM_WIDTH
PB_WIDTH = 2 * SW_WIDTH + 2 * SW_KV_WIDTH
PC_WIDTH = 4 * SB_WIDTH
PG_WIDTH = N_BRANCH * D_MODEL
PROJ_CHUNK = 256

SB_EXP_ZERO = 104.0
SB_STOP = SB_EXP_ZERO * 1.1
LOG2E = 1.4426950408889634


def _proj_chunks():
    segs = [
        (0, 1536, 0, 0, 1.0),
        (1536, 2048, 1, 0, HEAD_DIM ** -0.5 * LOG2E),
        (2048, 2304, 1, 2 * SW_WIDTH, 1.0),
        (2304, 2816, 1, SW_WIDTH, 1.0),
        (2816, 3328, 2, 0, SB_HEAD_DIM ** -0.5 * LOG2E),
        (3328, 4864, 2, SB_WIDTH, 1.0),
        (4864, 7936, 3, 0, 1.0),
    ]
    out = []
    for lo, hi, idx, dst, scale in segs:
        for c in range(lo, hi, PROJ_CHUNK):
            out.append((c, idx, dst + c - lo, scale))
    return out


def _sigmoid(x):
    return 1.0 / (1.0 + jnp.exp(-x))


def _silu(x):
    return x * _sigmoid(x)


def _rope(x, cos, sin):
    lane = lax.broadcasted_iota(jnp.int32, x.shape, 1)
    rot_fwd = (lane % HEAD_DIM) < (HEAD_DIM // 2)
    rot = jnp.where(rot_fwd, pltpu.roll(x, LANES - HEAD_DIM // 2, 1),
                    pltpu.roll(x, HEAD_DIM // 2, 1))
    return x * cos + rot * sin


def _front_kernel(x_ref, g_ref, w_ref, ws_ref, bs_ref, gv_ref, bm_ref, cos_ref, sin_ref,
                  ya_ref, pb_ref, pc_ref, pg_ref, pa_ref):
    x = x_ref[...]
    ms = jnp.mean(x * x, axis=-1, keepdims=True)
    h = (x * lax.rsqrt(ms + EPS) * g_ref[...]).astype(BF16)
    outs = (pa_ref, pb_ref, pc_ref, pg_ref)
    cos = cos_ref[...]
    sin = sin_ref[...]

    def emit(chunk):
        src, idx, dst, scale = chunk
        r = jnp.dot(h, w_ref[:, src:src + PROJ_CHUNK], preferred_element_type=F32)
        if scale != 1.0:
            r = r * scale
        if idx == 0:
            pa_ref[:, dst:dst + PROJ_CHUNK] = r
            return
        if idx == 1 and dst < SW_WIDTH:
            r = jnp.concatenate([_rope(r[:, :LANES], cos, sin), _rope(r[:, LANES:], cos, sin)],
                                axis=1)
        elif idx == 1 and dst == 2 * SW_WIDTH:
            r = jnp.concatenate([_rope(r[:, :LANES], cos, sin), r[:, LANES:]], axis=1)
        elif idx == 3:
            r = _sigmoid(r + bm_ref[:, dst:dst + PROJ_CHUNK])
        outs[idx][:, dst:dst + PROJ_CHUNK] = r.astype(BF16)

    chunks = _proj_chunks()
    for chunk in [c for c in chunks if c[1] == 0]:
        emit(chunk)
    row = lax.broadcasted_iota(jnp.int32, (BLOCK, BLOCK), 0)
    col = lax.broadcasted_iota(jnp.int32, (BLOCK, BLOCK), 1)
    causal = col <= row
    w = [jnp.where(causal, ws_ref[g], 0.0).astype(BF16) for g in range(GM_GROUPS)]
    for c in range(ROWS_PROJ // BLOCK):
        rows = slice(c * BLOCK, (c + 1) * BLOCK)
        v = pa_ref[rows, GM_WIDTH:2 * GM_WIDTH]
        mu = jnp.mean(v, axis=-1, keepdims=True)
        d = v - mu
        var = jnp.mean(d * d, axis=-1, keepdims=True)
        vn = (d * lax.rsqrt(var + EPS) * gv_ref[...]).astype(BF16)
        for g in range(GM_GROUPS):
            cols = slice(g * BLOCK, (g + 1) * BLOCK)
            mixed = jnp.dot(w[g], vn[:, cols], preferred_element_type=F32) + bs_ref[:, g:g + 1]
            u = pa_ref[rows, cols]
            gate = pa_ref[rows, 2 * GM_WIDTH + g * BLOCK:2 * GM_WIDTH + (g + 1) * BLOCK]
            ya_ref[rows, cols] = (u * mixed * _silu(gate)).astype(BF16)
    for chunk in [c for c in chunks if c[1] != 0]:
        emit(chunk)


def _front(x, g_pre, w_in, layer, w_s, b_s_t, g_v, b_merge, cos_t, sin_t, seq):
    m = x.shape[0]
    tm = ROWS_PROJ
    nj = seq // tm
    return pl.pallas_call(
        _front_kernel,
        grid=(m // tm,),
        in_specs=[
            pl.BlockSpec((tm, D_MODEL), lambda i: (i, 0)),
            pl.BlockSpec((1, D_MODEL), lambda i: (0, 0)),
            pl.BlockSpec((None, D_MODEL, IN_WIDTH), lambda i: (layer, 0, 0),
                         pipeline_mode=pl.Buffered(1)),
            pl.BlockSpec((GM_GROUPS, BLOCK, BLOCK), lambda i: (0, 0, 0)),
            pl.BlockSpec((BLOCK, GM_GROUPS), lambda i: (0, 0)),
            pl.BlockSpec((1, GM_WIDTH), lambda i: (0, 0)),
            pl.BlockSpec((1, PG_WIDTH), lambda i: (0, 0)),
            pl.BlockSpec((tm, LANES), lambda i: (i % nj, 0)),
            pl.BlockSpec((tm, LANES), lambda i: (i % nj, 0)),
        ],
        out_specs=[
            pl.BlockSpec((tm, GM_WIDTH), lambda i: (i, 0)),
            pl.BlockSpec((tm, PB_WIDTH), lambda i: (i, 0)),
            pl.BlockSpec((tm, PC_WIDTH), lambda i: (i, 0)),
            pl.BlockSpec((tm, PG_WIDTH), lambda i: (i, 0)),
        ],
        out_shape=[
            jax.ShapeDtypeStruct((m, GM_WIDTH), BF16),
            jax.ShapeDtypeStruct((m, PB_WIDTH), BF16),
            jax.ShapeDtypeStruct((m, PC_WIDTH), BF16),
            jax.ShapeDtypeStruct((m, PG_WIDTH), BF16),
        ],
        scratch_shapes=[pltpu.VMEM((tm, PA_WIDTH), F32)],
        compiler_params=pltpu.CompilerParams(
            dimension_semantics=("arbitrary",), vmem_limit_bytes=VMEM_LIMIT),
        name="front",
    )(x, g_pre, w_in, w_s, b_s_t, g_v, b_merge, cos_t, sin_t)


def _swa_kernel(sinks_ref, q_ref, gate_ref, k_ref, v_ref, o_ref,
                kprev_ref, vprev_ref, sc_ref):
    j = pl.program_id(1)

    @pl.when(j == 0)
    def _():
        kprev_ref[...] = jnp.zeros(kprev_ref.shape, BF16)
        vprev_ref[...] = jnp.zeros(vprev_ref.shape, BF16)

    lane = lax.broadcasted_iota(jnp.int32, (BLOCK, LANES), 1)
    qrow = lax.broadcasted_iota(jnp.int32, (BLOCK, LANES), 0)
    lo = lane < HEAD_DIM
    from_prev = lane > qrow
    zero = jnp.zeros((BLOCK, LANES), F32)
    no_prev = jnp.where(j == 0, NEG, 0.0)
    lo_ones = jnp.where(lo, 1.0, 0.0).astype(BF16)
    hi_ones = jnp.where(lo, 0.0, 1.0).astype(BF16)
    ones2 = jnp.concatenate([lo_ones, lo_ones, hi_ones, hi_ones], axis=0)

    def placed(x):
        xr = pltpu.roll(x, HEAD_DIM, 1)
        return [jnp.where(lo, x, zero).astype(BF16), jnp.where(lo, zero, xr).astype(BF16),
                jnp.where(lo, xr, zero).astype(BF16), jnp.where(lo, zero, x).astype(BF16)]

    kp = [kprev_ref[n] for n in range(2 * SW_KV_HEADS)]
    vp = [vprev_ref[n] for n in range(2 * SW_KV_HEADS)]
    nblk = ROWS_SWA // BLOCK
    v2s = {}
    for b in range(nblk):
        rows = slice(b * BLOCK, (b + 1) * BLOCK)
        kc = placed(k_ref[rows, :].astype(F32))
        vc = placed(v_ref[rows, :].astype(F32))
        for g in range(SW_KV_HEADS):
            v2 = jnp.concatenate([vp[2 * g], vc[2 * g], vp[2 * g + 1], vc[2 * g + 1]], axis=0)
            v2s[(b, g)] = jnp.concatenate([v2, ones2], axis=1)
        for pr in range(SW_HEADS // 2):
            g = pr // 2
            cols = slice(pr * LANES, (pr + 1) * LANES)
            k2 = jnp.concatenate([kp[2 * g], kc[2 * g], kp[2 * g + 1], kc[2 * g + 1]], axis=0)
            s = lax.dot_general(q_ref[rows, cols], k2, (((1,), (1,)), ((), ())),
                                preferred_element_type=F32)
            for hh in range(2):
                s_prev = s[:, 2 * hh * BLOCK:(2 * hh + 1) * BLOCK]
                s_cur = s[:, (2 * hh + 1) * BLOCK:(2 * hh + 2) * BLOCK]
                if b == 0:
                    s_prev = s_prev + no_prev
                sc_ref[b, pr, :, hh * BLOCK:(hh + 1) * BLOCK] = jnp.where(from_prev, s_prev, s_cur)
        kp, vp = kc, vc
    for n in range(2 * SW_KV_HEADS):
        kprev_ref[n] = kp[n]
        vprev_ref[n] = vp[n]
    for b in range(nblk):
        rows = slice(b * BLOCK, (b + 1) * BLOCK)
        for pr in range(SW_HEADS // 2):
            g = pr // 2
            cols = slice(pr * LANES, (pr + 1) * LANES)
            parts = []
            tops = []
            for hh in range(2):
                sc = sc_ref[b, pr, :, hh * BLOCK:(hh + 1) * BLOCK]
                m = jnp.max(sc, axis=-1, keepdims=True)
                p = jnp.exp2(sc - m)
                tops.append(m)
                parts.append(jnp.where(from_prev, p, zero).astype(BF16))
                parts.append(jnp.where(from_prev, zero, p).astype(BF16))
            r = jnp.dot(jnp.concatenate(parts, axis=1), v2s[(b, g)],
                        preferred_element_type=F32)
            sink = jnp.where(lo[0:1, :], sinks_ref[2 * pr], sinks_ref[2 * pr + 1]) * LOG2E
            denom = r[:, LANES:] + jnp.exp2(sink - jnp.where(lo, tops[0], tops[1]))
            o = r[:, :LANES] / denom
            o_ref[rows, cols] = (o * _silu(gate_ref[rows, cols].astype(F32))).astype(BF16)


def _swa(pb, sinks, batch, seq):
    m = pb.shape[0]
    tq = ROWS_SWA
    nj = seq // tq
    kcol = 2 * SW_WIDTH // SW_KV_WIDTH
    return pl.pallas_call(
        _swa_kernel,
        grid=(batch, nj),
        in_specs=[
            pl.BlockSpec(memory_space=pltpu.SMEM),
            pl.BlockSpec((tq, SW_WIDTH), lambda b, j: (b * nj + j, 0)),
            pl.BlockSpec((tq, SW_WIDTH), lambda b, j: (b * nj + j, 1)),
            pl.BlockSpec((tq, SW_KV_WIDTH), lambda b, j: (b * nj + j, kcol)),
            pl.BlockSpec((tq, SW_KV_WIDTH), lambda b, j: (b * nj + j, kcol + 1)),
        ],
        out_specs=pl.BlockSpec((tq, SW_WIDTH), lambda b, j: (b * nj + j, 0)),
        out_shape=jax.ShapeDtypeStruct((m, SW_WIDTH), BF16),
        scratch_shapes=[
            pltpu.VMEM((2 * SW_KV_HEADS, BLOCK, LANES), BF16),
            pltpu.VMEM((2 * SW_KV_HEADS, BLOCK, LANES), BF16),
            pltpu.VMEM((ROWS_SWA // BLOCK, SW_HEADS // 2, BLOCK, 2 * BLOCK), F32),
        ],
        compiler_params=pltpu.CompilerParams(
            dimension_semantics=("arbitrary", "arbitrary"), vmem_limit_bytes=VMEM_LIMIT),
        name="swa",
    )(sinks, pb, pb, pb, pb)


def _sb_kernel(q_ref, k_ref, v_ref, gate_ref, o_ref, suf_ref, acc_ref, carry_ref,
               z_ref, hl_ref, a_ref, zt_ref, hlt_ref, at_ref):
    c = pl.program_id(2)
    row = lax.broadcasted_iota(jnp.int32, (BLOCK, BLOCK), 0)
    col = lax.broadcasted_iota(jnp.int32, (BLOCK, BLOCK), 1)
    before = col < row
    one_suffix = jnp.where(row >= col, 1.0, 0.0).astype(BF16)
    half = jnp.concatenate([one_suffix, jnp.ones((BLOCK, BLOCK), BF16)], axis=1)
    suf_ref[...] = jnp.concatenate([half, half], axis=0)
    groups = ROWS_SB // (SB_GROUP * BLOCK)
    contract_lanes = (((1,), (1,)), ((), ()))
    stop = SB_STOP * LOG2E

    def fail_of(zl):
        return jnp.maximum(zl, 0.0) + jnp.log2(1.0 + jnp.exp2(-jnp.abs(zl)))

    def split(f):
        hi = f.astype(BF16)
        lo = (f - hi.astype(F32)).astype(BF16)
        return jnp.concatenate([hi, lo], axis=1)

    def suffix_and_total(hl):
        r = jnp.dot(hl, suf_ref[...], preferred_element_type=F32)
        return r[:, :BLOCK], r[:, BLOCK:]

    def one_tile(q, m, carry, acc, diag):
        k0 = pl.multiple_of(m * BLOCK, BLOCK)
        zl = lax.dot_general(q, k_ref[pl.ds(k0, BLOCK), :], contract_lanes,
                             preferred_element_type=F32)
        f = fail_of(zl)
        if diag:
            f = jnp.where(before, f, 0.0)
        cs, tot = suffix_and_total(split(f))
        a = jnp.exp2(zl - cs - carry)
        if diag:
            a = jnp.where(before, a, 0.0)
        acc = acc + jnp.dot(a.astype(BF16), v_ref[pl.ds(k0, BLOCK), :],
                            preferred_element_type=F32)
        return carry + tot, acc

    def rows_of(blk):
        return pl.ds(pl.multiple_of(blk * BLOCK, BLOCK), BLOCK)

    def finish(blk, acc):
        rows = rows_of(blk)
        o_ref[rows, :] = (acc * _silu(gate_ref[rows, :].astype(F32))).astype(BF16)

    def walk_block(blk, start_offset):
        q = q_ref[rows_of(blk), :]

        def cond(st):
            m, least, _, _ = st
            return jnp.logical_and(m >= 0, least < stop)

        def body(st):
            m, _, carry, acc = st
            carry, acc = one_tile(q, m, carry, acc, False)
            return m - 1, jnp.min(carry), carry, acc

        carry = carry_ref[blk]
        start = c * (ROWS_SB // BLOCK) + blk - start_offset
        _, _, _, acc = lax.while_loop(cond, body, (start, jnp.min(carry), carry, acc_ref[blk]))
        finish(blk, acc)

    def walk_group(gl):
        def walk(ib, _):
            walk_block(gl * SB_GROUP + ib, 2)
            return 0
        lax.fori_loop(0, SB_GROUP, walk, 0)

    def group(gl, least_prev):
        g0 = c * (ROWS_SB // BLOCK) + gl * SB_GROUP
        top = SB_TOP_ROWS
        spans, tops = [], []
        for ib in range(SB_GROUP):
            blk = gl * SB_GROUP + ib
            k1 = pl.multiple_of(jnp.maximum(g0 + ib - 1, 0) * BLOCK, BLOCK)
            k2 = pl.multiple_of(jnp.maximum(g0 + ib - 2, 0) * BLOCK, BLOCK)
            spans.append(pl.ds(k1, 2 * BLOCK))
            tops.append(pl.ds(k2, BLOCK))
            q = q_ref[rows_of(blk), :]
            zl = lax.dot_general(q, k_ref[spans[ib], :], contract_lanes,
                                 preferred_element_type=F32)
            zt = lax.dot_general(q[:top], k_ref[tops[ib], :], contract_lanes,
                                 preferred_element_type=F32)
            f = fail_of(zl)
            z_ref[ib] = zl
            zt_ref[ib] = zt
            hl_ref[ib, 0] = split(f[:, :BLOCK])
            hl_ref[ib, 1] = split(jnp.where(before, f[:, BLOCK:], 0.0))
            hlt_ref[ib] = split(fail_of(zt))
        least = None
        for ib in range(SB_GROUP):
            cs, carry = suffix_and_total(hl_ref[ib, 1])
            a_ref[ib, :, BLOCK:] = jnp.where(
                before, jnp.exp2(z_ref[ib, :, BLOCK:] - cs), 0.0).astype(BF16)
            cs, tot = suffix_and_total(hl_ref[ib, 0])
            a_ref[ib, :, :BLOCK] = jnp.exp2(z_ref[ib, :, :BLOCK] - cs - carry).astype(BF16)
            carry = carry + tot
            cs, tot = suffix_and_total(hlt_ref[ib])
            at_ref[ib] = jnp.exp2(zt_ref[ib] - cs - carry[:top]).astype(BF16)
            carry_ref[gl * SB_GROUP + ib] = carry
            reach = jnp.concatenate([carry[:top] + tot, carry[top:]], axis=0)
            least = reach if least is None else jnp.minimum(least, reach)
        for ib in range(SB_GROUP):
            blk = gl * SB_GROUP + ib
            acc = jnp.dot(a_ref[ib], v_ref[spans[ib], :], preferred_element_type=F32)
            acc_top = jnp.dot(at_ref[ib], v_ref[tops[ib], :], preferred_element_type=F32)
            acc_ref[blk] = acc
            finish(blk, jnp.concatenate([acc[:top] + acc_top, acc[top:]], axis=0))

        @pl.when(jnp.min(least_prev) < stop)
        def _():
            walk_group(gl - 1)

        return least

    settled = jnp.full((BLOCK, BLOCK), 2.0 * stop, F32)
    least_last = lax.fori_loop(0, groups, group, settled)

    @pl.when(jnp.min(least_last) < stop)
    def _():
        walk_group(groups - 1)

    @pl.when(c == 0)
    def _():
        def redo(ib, _):
            carry, acc = one_tile(q_ref[rows_of(ib), :], ib, jnp.zeros((BLOCK, BLOCK), F32),
                                  jnp.zeros((BLOCK, SB_HEAD_DIM), F32), True)
            carry_ref[ib] = carry
            acc_ref[ib] = acc
            walk_block(ib, 1)
            return 0
        lax.fori_loop(0, 2, redo, 0)


def _sb(pc, batch, seq):
    m = pc.shape[0]
    tq = ROWS_SB
    nc = seq // tq
    hb = SB_WIDTH // SB_HEAD_DIM
    return pl.pallas_call(
        _sb_kernel,
        grid=(batch, SB_HEADS, nc),
        in_specs=[
            pl.BlockSpec((tq, SB_HEAD_DIM), lambda b, h, c: (b * nc + c, h)),
            pl.BlockSpec((seq, SB_HEAD_DIM), lambda b, h, c: (b, hb + h)),
            pl.BlockSpec((seq, SB_HEAD_DIM), lambda b, h, c: (b, 2 * hb + h)),
            pl.BlockSpec((tq, SB_HEAD_DIM), lambda b, h, c: (b * nc + c, 3 * hb + h)),
        ],
        out_specs=pl.BlockSpec((tq, SB_HEAD_DIM), lambda b, h, c: (b * nc + c, h)),
        out_shape=jax.ShapeDtypeStruct((m, SB_WIDTH), BF16),
        scratch_shapes=[
            pltpu.VMEM((2 * BLOCK, 2 * BLOCK), BF16),
            pltpu.VMEM((ROWS_SB // BLOCK, BLOCK, SB_HEAD_DIM), F32),
            pltpu.VMEM((ROWS_SB // BLOCK, BLOCK, BLOCK), F32),
            pltpu.VMEM((SB_GROUP, BLOCK, 2 * BLOCK), F32),
            pltpu.VMEM((SB_GROUP, 2, BLOCK, 2 * BLOCK), BF16),
            pltpu.VMEM((SB_GROUP, BLOCK, 2 * BLOCK), BF16),
            pltpu.VMEM((SB_GROUP, SB_TOP_ROWS, BLOCK), F32),
            pltpu.VMEM((SB_GROUP, SB_TOP_ROWS, 2 * BLOCK), BF16),
            pltpu.VMEM((SB_GROUP, SB_TOP_ROWS, BLOCK), BF16),
        ],
        compiler_params=pltpu.CompilerParams(
            dimension_semantics=("arbitrary", "arbitrary", "arbitrary"),
            vmem_limit_bytes=VMEM_LIMIT),
        name="sb",
    )(pc, pc, pc, pc)


def _merge_kernel(ya_ref, yb_ref, yc_ref, pg_ref, x_ref, wa_ref, wb_ref, wc_ref,
                  wo_ref, gp_ref, o_ref):
    merged = None
    for n, (y_ref, w_ref) in enumerate(((ya_ref, wa_ref), (yb_ref, wb_ref), (yc_ref, wc_ref))):
        cols = slice(n * D_MODEL, (n + 1) * D_MODEL)
        term = pg_ref[:, cols].astype(F32) * jnp.dot(y_ref[...], w_ref[...],
                                                     preferred_element_type=F32)
        merged = term if merged is None else merged + term
    out = jnp.dot(merged.astype(BF16), wo_ref[...], preferred_element_type=F32)
    ms = jnp.mean(out * out, axis=-1, keepdims=True)
    o_ref[...] = x_ref[...] + out * lax.rsqrt(ms + EPS) * gp_ref[...]


def _merge(ya, yb, yc, pg, x, wa, wb, wc, w_out, layer, g_post):
    m = x.shape[0]
    tm = ROWS_MERGE
    row = lambda width: pl.BlockSpec((tm, width), lambda i: (i, 0))
    full = lambda a: pl.BlockSpec(a.shape, lambda i: (0, 0))
    of_layer = lambda a: pl.BlockSpec((None,) + a.shape[1:], lambda i: (layer, 0, 0))
    return pl.pallas_call(
        _merge_kernel,
        grid=(m // tm,),
        in_specs=[row(GM_WIDTH), row(SW_WIDTH), row(SB_WIDTH), row(PG_WIDTH), row(D_MODEL),
                  of_layer(wa), of_layer(wb), of_layer(wc), of_layer(w_out), full(g_post)],
        out_specs=row(D_MODEL),
        out_shape=jax.ShapeDtypeStruct((m, D_MODEL), F32),
        compiler_params=pltpu.CompilerParams(
            dimension_semantics=("arbitrary",), vmem_limit_bytes=VMEM_LIMIT),
        name="merge",
    )(ya, yb, yc, pg, x, wa, wb, wc, w_out, g_post)


def _rope_tables(seq):
    half = HEAD_DIM // 2
    freqs = ROPE_THETA ** (-jnp.arange(half, dtype=F32) / half)
    ang = jnp.arange(seq).astype(F32)[:, None] * freqs[None, :]
    cos = jnp.tile(jnp.cos(ang), (1, LANES // half))
    sin = jnp.tile(jnp.concatenate([-jnp.sin(ang), jnp.sin(ang)], axis=-1), (1, LANES // HEAD_DIM))
    return cos, sin


def kernel(x, w_in, gm_w_s, gm_b_s, gm_norm_gain, sw_sinks, w_branch_a, w_branch_b, w_branch_c,
           b_merge, w_out, g_pre, g_post):
    batch, seq, _ = x.shape
    assert seq % ROWS_SB == 0 and (batch * seq) % ROWS_PROJ == 0
    depth = w_in.shape[0]
    cos_t, sin_t = _rope_tables(seq)
    xf = x.reshape(batch * seq, D_MODEL)
    w_in, w_out = w_in.astype(BF16), w_out.astype(BF16)
    wa, wb, wc = w_branch_a.astype(BF16), w_branch_b.astype(BF16), w_branch_c.astype(BF16)
    for l in range(depth):
        ya, pb, pc, pg = _front(xf, g_pre[l][None, :], w_in, l, gm_w_s[l], gm_b_s[l].T,
                                gm_norm_gain[l][None, :], b_merge[l].reshape(1, PG_WIDTH),
                                cos_t, sin_t, seq)
        yb = _swa(pb, sw_sinks[l], batch, seq)
        yc = _sb(pc, batch, seq)
        xf = _merge(ya, yb, yc, pg, xf, wa, wb, wc, w_out, l, g_post[l][None, :])
    return xf.reshape(batch, seq, D_MODEL)
```

```python
import jax
import jax.numpy as jnp
from jax import lax
from jax.experimental import pallas as pl
from jax.experimental.pallas import tpu as pltpu

F32 = jnp.float32
BF16 = jnp.bfloat16

D_MODEL = 1024
BLOCK = 128
EPS = 1e-6
NEG = -1e30
GM_GROUPS = 4
GM_WIDTH = 512
HEAD_DIM = 64
SW_HEADS = 8
SW_KV_HEADS = 2
SW_WIDTH = 512
SW_KV_WIDTH = 128
ROPE_THETA = 10000.0
SB_HEADS = 4
SB_HEAD_DIM = 128
SB_WIDTH = 512
N_BRANCH = 3
IN_WIDTH = 7936

V7X_VMEM_BYTES = 64 * 1024 * 1024
LANES = 128
VMEM_LIMIT = V7X_VMEM_BYTES - 8 * 1024 * 1024

ROWS_PROJ = 512
ROWS_SWA = 512
ROWS_SB = 2048
ROWS_MERGE = 1024
SB_GROUP = 16
SB_TOP_ROWS = 64

PA_WIDTH = 3 * GM_WIDTH
PB_WIDTH = 2 * SW_WIDTH + 2 * SW_KV_WIDTH
PC_WIDTH = 4 * SB_WIDTH
PG_WIDTH = N_BRANCH * D_MODEL
PROJ_CHUNK = 256

SB_EXP_ZERO = 104.0
SB_STOP = SB_EXP_ZERO * 1.1
LOG2E = 1.4426950408889634


def _proj_chunks():
    segs = [
        (0, 1536, 0, 0, 1.0),
        (1536, 2048, 1, 0, HEAD_DIM ** -0.5 * LOG2E),
        (2048, 2304, 1, 2 * SW_WIDTH, 1.0),
        (2304, 2816, 1, SW_WIDTH, 1.0),
        (2816, 3328, 2, 0, SB_HEAD_DIM ** -0.5 * LOG2E),
        (3328, 4864, 2, SB_WIDTH, 1.0),
        (4864, 7936, 3, 0, 1.0),
    ]
    out = []
    for lo, hi, idx, dst, scale in segs:
        for c in range(lo, hi, PROJ_CHUNK):
            out.append((c, idx, dst + c - lo, scale))
    return out


def _sigmoid(x):
    return 1.0 / (1.0 + jnp.exp(-x))


def _silu(x):
    return x * _sigmoid(x)


def _rope(x, cos, sin):
    lane = lax.broadcasted_iota(jnp.int32, x.shape, 1)
    rot_fwd = (lane % HEAD_DIM) < (HEAD_DIM // 2)
    rot = jnp.where(rot_fwd, pltpu.roll(x, LANES - HEAD_DIM // 2, 1),
                    pltpu.roll(x, HEAD_DIM // 2, 1))
    return x * cos + rot * sin


def _front_kernel(x_ref, g_ref, w_ref, ws_ref, bs_ref, gv_ref, bm_ref, cos_ref, sin_ref,
                  ya_ref, pb_ref, pc_ref, pg_ref, pa_ref):
    x = x_ref[...]
    ms = jnp.mean(x * x, axis=-1, keepdims=True)
    h = (x * lax.rsqrt(ms + EPS) * g_ref[...]).astype(BF16)
    outs = (pa_ref, pb_ref, pc_ref, pg_ref)
    cos = cos_ref[...]
    sin = sin_ref[...]

    def emit(chunk):
        src, idx, dst, scale = chunk
        r = jnp.dot(h, w_ref[:, src:src + PROJ_CHUNK], preferred_element_type=F32)
        if scale != 1.0:
            r = r * scale
        if idx == 0:
            pa_ref[:, dst:dst + PROJ_CHUNK] = r
            return
        if idx == 1 and dst < SW_WIDTH:
            r = jnp.concatenate([_rope(r[:, :LANES], cos, sin), _rope(r[:, LANES:], cos, sin)],
                                axis=1)
        elif idx == 1 and dst == 2 * SW_WIDTH:
            r = jnp.concatenate([_rope(r[:, :LANES], cos, sin), r[:, LANES:]], axis=1)
        elif idx == 3:
            r = _sigmoid(r + bm_ref[:, dst:dst + PROJ_CHUNK])
        outs[idx][:, dst:dst + PROJ_CHUNK] = r.astype(BF16)

    chunks = _proj_chunks()
    for chunk in [c for c in chunks if c[1] == 0]:
        emit(chunk)
    row = lax.broadcasted_iota(jnp.int32, (BLOCK, BLOCK), 0)
    col = lax.broadcasted_iota(jnp.int32, (BLOCK, BLOCK), 1)
    causal = col <= row
    w = [jnp.where(causal, ws_ref[g], 0.0).astype(BF16) for g in range(GM_GROUPS)]
    for c in range(ROWS_PROJ // BLOCK):
        rows = slice(c * BLOCK, (c + 1) * BLOCK)
        v = pa_ref[rows, GM_WIDTH:2 * GM_WIDTH]
        mu = jnp.mean(v, axis=-1, keepdims=True)
        d = v - mu
        var = jnp.mean(d * d, axis=-1, keepdims=True)
        vn = (d * lax.rsqrt(var + EPS) * gv_ref[...]).astype(BF16)
        for g in range(GM_GROUPS):
            cols = slice(g * BLOCK, (g + 1) * BLOCK)
            mixed = jnp.dot(w[g], vn[:, cols], preferred_element_type=F32) + bs_ref[:, g:g + 1]
            u = pa_ref[rows, cols]
            gate = pa_ref[rows, 2 * GM_WIDTH + g * BLOCK:2 * GM_WIDTH + (g + 1) * BLOCK]
            ya_ref[rows, cols] = (u * mixed * _silu(gate)).astype(BF16)
    for chunk in [c for c in chunks if c[1] != 0]:
        emit(chunk)


def _front(x, g_pre, w_in, layer, w_s, b_s_t, g_v, b_merge, cos_t, sin_t, seq):
    m = x.shape[0]
    tm = ROWS_PROJ
    nj = seq // tm
    return pl.pallas_call(
        _front_kernel,
        grid=(m // tm,),
        in_specs=[
            pl.BlockSpec((tm, D_MODEL), lambda i: (i, 0)),
            pl.BlockSpec((1, D_MODEL), lambda i: (0, 0)),
            pl.BlockSpec((None, D_MODEL, IN_WIDTH), lambda i: (layer, 0, 0),
                         pipeline_mode=pl.Buffered(1)),
            pl.BlockSpec((GM_GROUPS, BLOCK, BLOCK), lambda i: (0, 0, 0)),
            pl.BlockSpec((BLOCK, GM_GROUPS), lambda i: (0, 0)),
            pl.BlockSpec((1, GM_WIDTH), lambda i: (0, 0)),
            pl.BlockSpec((1, PG_WIDTH), lambda i: (0, 0)),
            pl.BlockSpec((tm, LANES), lambda i: (i % nj, 0)),
            pl.BlockSpec((tm, LANES), lambda i: (i % nj, 0)),
        ],
        out_specs=[
            pl.BlockSpec((tm, GM_WIDTH), lambda i: (i, 0)),
            pl.BlockSpec((tm, PB_WIDTH), lambda i: (i, 0)),
            pl.BlockSpec((tm, PC_WIDTH), lambda i: (i, 0)),
            pl.BlockSpec((tm, PG_WIDTH), lambda i: (i, 0)),
        ],
        out_shape=[
            jax.ShapeDtypeStruct((m, GM_WIDTH), BF16),
            jax.ShapeDtypeStruct((m, PB_WIDTH), BF16),
            jax.ShapeDtypeStruct((m, PC_WIDTH), BF16),
            jax.ShapeDtypeStruct((m, PG_WIDTH), BF16),
        ],
        scratch_shapes=[pltpu.VMEM((tm, PA_WIDTH), F32)],
        compiler_params=pltpu.CompilerParams(
            dimension_semantics=("arbitrary",), vmem_limit_bytes=VMEM_LIMIT),
        name="front",
    )(x, g_pre, w_in, w_s, b_s_t, g_v, b_merge, cos_t, sin_t)


def _swa_kernel(sinks_ref, q_ref, gate_ref, k_ref, v_ref, o_ref,
                kprev_ref, vprev_ref, sc_ref):
    j = pl.program_id(1)

    @pl.when(j == 0)
    def _():
        kprev_ref[...] = jnp.zeros(kprev_ref.shape, BF16)
        vprev_ref[...] = jnp.zeros(vprev_ref.shape, BF16)

    lane = lax.broadcasted_iota(jnp.int32, (BLOCK, LANES), 1)
    qrow = lax.broadcasted_iota(jnp.int32, (BLOCK, LANES), 0)
    lo = lane < HEAD_DIM
    from_prev = lane > qrow
    zero = jnp.zeros((BLOCK, LANES), F32)
    no_prev = jnp.where(j == 0, NEG, 0.0)
    lo_ones = jnp.where(lo, 1.0, 0.0).astype(BF16)
    hi_ones = jnp.where(lo, 0.0, 1.0).astype(BF16)
    ones2 = jnp.concatenate([lo_ones, lo_ones, hi_ones, hi_ones], axis=0)

    def placed(x):
        xr = pltpu.roll(x, HEAD_DIM, 1)
        return [jnp.where(lo, x, zero).astype(BF16), jnp.where(lo, zero, xr).astype(BF16),
                jnp.where(lo, xr, zero).astype(BF16), jnp.where(lo, zero, x).astype(BF16)]

    kp = [kprev_ref[n] for n in range(2 * SW_KV_HEADS)]
    vp = [vprev_ref[n] for n in range(2 * SW_KV_HEADS)]
    nblk = ROWS_SWA // BLOCK
    v2s = {}
    for b in range(nblk):
        rows = slice(b * BLOCK, (b + 1) * BLOCK)
        kc = placed(k_ref[rows, :].astype(F32))
        vc = placed(v_ref[rows, :].astype(F32))
        for g in range(SW_KV_HEADS):
            v2 = jnp.concatenate([vp[2 * g], vc[2 * g], vp[2 * g + 1], vc[2 * g + 1]], axis=0)
            v2s[(b, g)] = jnp.concatenate([v2, ones2], axis=1)
        for pr in range(SW_HEADS // 2):
            g = pr // 2
            cols = slice(pr * LANES, (pr + 1) * LANES)
            k2 = jnp.concatenate([kp[2 * g], kc[2 * g], kp[2 * g + 1], kc[2 * g + 1]], axis=0)
            s = lax.dot_general(q_ref[rows, cols], k2, (((1,), (1,)), ((), ())),
                                preferred_element_type=F32)
            for hh in range(2):
                s_prev = s[:, 2 * hh * BLOCK:(2 * hh + 1) * BLOCK]
                s_cur = s[:, (2 * hh + 1) * BLOCK:(2 * hh + 2) * BLOCK]
                if b == 0:
                    s_prev = s_prev + no_prev
                sc_ref[b, pr, :, hh * BLOCK:(hh + 1) * BLOCK] = jnp.where(from_prev, s_prev, s_cur)
        kp, vp = kc, vc
    for n in range(2 * SW_KV_HEADS):
        kprev_ref[n] = kp[n]
        vprev_ref[n] = vp[n]
    for b in range(nblk):
        rows = slice(b * BLOCK, (b + 1) * BLOCK)
        for pr in range(SW_HEADS // 2):
            g = pr // 2
            cols = slice(pr * LANES, (pr + 1) * LANES)
            parts = []
            tops = []
            for hh in range(2):
                sc = sc_ref[b, pr, :, hh * BLOCK:(hh + 1) * BLOCK]
                m = jnp.max(sc, axis=-1, keepdims=True)
                p = jnp.exp2(sc - m)
                tops.append(m)
                parts.append(jnp.where(from_prev, p, zero).astype(BF16))
                parts.append(jnp.where(from_prev, zero, p).astype(BF16))
            r = jnp.dot(jnp.concatenate(parts, axis=1), v2s[(b, g)],
                        preferred_element_type=F32)
            sink = jnp.where(lo[0:1, :], sinks_ref[2 * pr], sinks_ref[2 * pr + 1]) * LOG2E
            denom = r[:, LANES:] + jnp.exp2(sink - jnp.where(lo, tops[0], tops[1]))
            o = r[:, :LANES] / denom
            o_ref[rows, cols] = (o * _silu(gate_ref[rows, cols].astype(F32))).astype(BF16)


def _swa(pb, sinks, batch, seq):
    m = pb.shape[0]
    tq = ROWS_SWA
    nj = seq // tq
    kcol = 2 * SW_WIDTH // SW_KV_WIDTH
    return pl.pallas_call(
        _swa_kernel,
        grid=(batch, nj),
        in_specs=[
            pl.BlockSpec(memory_space=pltpu.SMEM),
            pl.BlockSpec((tq, SW_WIDTH), lambda b, j: (b * nj + j, 0)),
            pl.BlockSpec((tq, SW_WIDTH), lambda b, j: (b * nj + j, 1)),
            pl.BlockSpec((tq, SW_KV_WIDTH), lambda b, j: (b * nj + j, kcol)),
            pl.BlockSpec((tq, SW_KV_WIDTH), lambda b, j: (b * nj + j, kcol + 1)),
        ],
        out_specs=pl.BlockSpec((tq, SW_WIDTH), lambda b, j: (b * nj + j, 0)),
        out_shape=jax.ShapeDtypeStruct((m, SW_WIDTH), BF16),
        scratch_shapes=[
            pltpu.VMEM((2 * SW_KV_HEADS, BLOCK, LANES), BF16),
            pltpu.VMEM((2 * SW_KV_HEADS, BLOCK, LANES), BF16),
            pltpu.VMEM((ROWS_SWA // BLOCK, SW_HEADS // 2, BLOCK, 2 * BLOCK), F32),
        ],
        compiler_params=pltpu.CompilerParams(
            dimension_semantics=("arbitrary", "arbitrary"), vmem_limit_bytes=VMEM_LIMIT),
        name="swa",
    )(sinks, pb, pb, pb, pb)


def _sb_kernel(q_ref, k_ref, v_ref, gate_ref, o_ref, suf_ref, acc_ref, carry_ref,
               z_ref, hl_ref, a_ref, zt_ref, hlt_ref, at_ref):
    c = pl.program_id(2)
    row = lax.broadcasted_iota(jnp.int32, (BLOCK, BLOCK), 0)
    col = lax.broadcasted_iota(jnp.int32, (BLOCK, BLOCK), 1)
    before = col < row
    one_suffix = jnp.where(row >= col, 1.0, 0.0).astype(BF16)
    half = jnp.concatenate([one_suffix, jnp.ones((BLOCK, BLOCK), BF16)], axis=1)
    suf_ref[...] = jnp.concatenate([half, half], axis=0)
    groups = ROWS_SB // (SB_GROUP * BLOCK)
    contract_lanes = (((1,), (1,)), ((), ()))
    stop = SB_STOP * LOG2E

    def fail_of(zl):
        return jnp.maximum(zl, 0.0) + jnp.log2(1.0 + jnp.exp2(-jnp.abs(zl)))

    def split(f):
        hi = f.astype(BF16)
        lo = (f - hi.astype(F32)).astype(BF16)
        return jnp.concatenate([hi, lo], axis=1)

    def suffix_and_total(hl):
        r = jnp.dot(hl, suf_ref[...], preferred_element_type=F32)
        return r[:, :BLOCK], r[:, BLOCK:]

    def one_tile(q, m, carry, acc, diag):
        k0 = pl.multiple_of(m * BLOCK, BLOCK)
        zl = lax.dot_general(q, k_ref[pl.ds(k0, BLOCK), :], contract_lanes,
                             preferred_element_type=F32)
        f = fail_of(zl)
        if diag:
            f = jnp.where(before, f, 0.0)
        cs, tot = suffix_and_total(split(f))
        a = jnp.exp2(zl - cs - carry)
        if diag:
            a = jnp.where(before, a, 0.0)
        acc = acc + jnp.dot(a.astype(BF16), v_ref[pl.ds(k0, BLOCK), :],
                            preferred_element_type=F32)
        return carry + tot, acc

    def rows_of(blk):
        return pl.ds(pl.multiple_of(blk * BLOCK, BLOCK), BLOCK)

    def finish(blk, acc):
        rows = rows_of(blk)
        o_ref[rows, :] = (acc * _silu(gate_ref[rows, :].astype(F32))).astype(BF16)

    def walk_block(blk, start_offset):
        q = q_ref[rows_of(blk), :]

        def cond(st):
            m, least, _, _ = st
            return jnp.logical_and(m >= 0, least < stop)

        def body(st):
            m, _, carry, acc = st
            carry, acc = one_tile(q, m, carry, acc, False)
            return m - 1, jnp.min(carry), carry, acc

        carry = carry_ref[blk]
        start = c * (ROWS_SB // BLOCK) + blk - start_offset
        _, _, _, acc = lax.while_loop(cond, body, (start, jnp.min(carry), carry, acc_ref[blk]))
        finish(blk, acc)

    def walk_group(gl):
        def walk(ib, _):
            walk_block(gl * SB_GROUP + ib, 2)
            return 0
        lax.fori_loop(0, SB_GROUP, walk, 0)

    def group(gl, least_prev):
        g0 = c * (ROWS_SB // BLOCK) + gl * SB_GROUP
        top = SB_TOP_ROWS
        spans, tops = [], []
        for ib in range(SB_GROUP):
            blk = gl * SB_GROUP + ib
            k1 = pl.multiple_of(jnp.maximum(g0 + ib - 1, 0) * BLOCK, BLOCK)
            k2 = pl.multiple_of(jnp.maximum(g0 + ib - 2, 0) * BLOCK, BLOCK)
            spans.append(pl.ds(k1, 2 * BLOCK))
            tops.append(pl.ds(k2, BLOCK))
            q = q_ref[rows_of(blk), :]
            zl = lax.dot_general(q, k_ref[spans[ib], :], contract_lanes,
                                 preferred_element_type=F32)
            zt = lax.dot_general(q[:top], k_ref[tops[ib], :], contract_lanes,
                                 preferred_element_type=F32)
            f = fail_of(zl)
            z_ref[ib] = zl
            zt_ref[ib] = zt
            hl_ref[ib, 0] = split(f[:, :BLOCK])
            hl_ref[ib, 1] = split(jnp.where(before, f[:, BLOCK:], 0.0))
            hlt_ref[ib] = split(fail_of(zt))
        least = None
        for ib in range(SB_GROUP):
            cs, carry = suffix_and_total(hl_ref[ib, 1])
            a_ref[ib, :, BLOCK:] = jnp.where(
                before, jnp.exp2(z_ref[ib, :, BLOCK:] - cs), 0.0).astype(BF16)
            cs, tot = suffix_and_total(hl_ref[ib, 0])
            a_ref[ib, :, :BLOCK] = jnp.exp2(z_ref[ib, :, :BLOCK] - cs - carry).astype(BF16)
            carry = carry + tot
            cs, tot = suffix_and_total(hlt_ref[ib])
            at_ref[ib] = jnp.exp2(zt_ref[ib] - cs - carry[:top]).astype(BF16)
            carry_ref[gl * SB_GROUP + ib] = carry
            reach = jnp.concatenate([carry[:top] + tot, carry[top:]], axis=0)
            least = reach if least is None else jnp.minimum(least, reach)
        for ib in range(SB_GROUP):
            blk = gl * SB_GROUP + ib
            acc = jnp.dot(a_ref[ib], v_ref[spans[ib], :], preferred_element_type=F32)
            acc_top = jnp.dot(at_ref[ib], v_ref[tops[ib], :], preferred_element_type=F32)
            acc_ref[blk] = acc
            finish(blk, jnp.concatenate([acc[:top] + acc_top, acc[top:]], axis=0))

        @pl.when(jnp.min(least_prev) < stop)
        def _():
            walk_group(gl - 1)

        return least

    settled = jnp.full((BLOCK, BLOCK), 2.0 * stop, F32)
    least_last = lax.fori_loop(0, groups, group, settled)

    @pl.when(jnp.min(least_last) < stop)
    def _():
        walk_group(groups - 1)

    @pl.when(c == 0)
    def _():
        def redo(ib, _):
            carry, acc = one_tile(q_ref[rows_of(ib), :], ib, jnp.zeros((BLOCK, BLOCK), F32),
                                  jnp.zeros((BLOCK, SB_HEAD_DIM), F32), True)
            carry_ref[ib] = carry
            acc_ref[ib] = acc
            walk_block(ib, 1)
            return 0
        lax.fori_loop(0, 2, redo, 0)


def _sb(pc, batch, seq):
    m = pc.shape[0]
    tq = ROWS_SB
    nc = seq // tq
    hb = SB_WIDTH // SB_HEAD_DIM
    return pl.pallas_call(
        _sb_kernel,
        grid=(batch, SB_HEADS, nc),
        in_specs=[
            pl.BlockSpec((tq, SB_HEAD_DIM), lambda b, h, c: (b * nc + c, h)),
            pl.BlockSpec((seq, SB_HEAD_DIM), lambda b, h, c: (b, hb + h)),
            pl.BlockSpec((seq, SB_HEAD_DIM), lambda b, h, c: (b, 2 * hb + h)),
            pl.BlockSpec((tq, SB_HEAD_DIM), lambda b, h, c: (b * nc + c, 3 * hb + h)),
        ],
        out_specs=pl.BlockSpec((tq, SB_HEAD_DIM), lambda b, h, c: (b * nc + c, h)),
        out_shape=jax.ShapeDtypeStruct((m, SB_WIDTH), BF16),
        scratch_shapes=[
            pltpu.VMEM((2 * BLOCK, 2 * BLOCK), BF16),
            pltpu.VMEM((ROWS_SB // BLOCK, BLOCK, SB_HEAD_DIM), F32),
            pltpu.VMEM((ROWS_SB // BLOCK, BLOCK, BLOCK), F32),
            pltpu.VMEM((SB_GROUP, BLOCK, 2 * BLOCK), F32),
            pltpu.VMEM((SB_GROUP, 2, BLOCK, 2 * BLOCK), BF16),
            pltpu.VMEM((SB_GROUP, BLOCK, 2 * BLOCK), BF16),
            pltpu.VMEM((SB_GROUP, SB_TOP_ROWS, BLOCK), F32),
            pltpu.VMEM((SB_GROUP, SB_TOP_ROWS, 2 * BLOCK), BF16),
            pltpu.VMEM((SB_GROUP, SB_TOP_ROWS, BLOCK), BF16),
        ],
        compiler_params=pltpu.CompilerParams(
            dimension_semantics=("arbitrary", "arbitrary", "arbitrary"),
            vmem_limit_bytes=VMEM_LIMIT),
        name="sb",
    )(pc, pc, pc, pc)


def _merge_kernel(ya_ref, yb_ref, yc_ref, pg_ref, x_ref, wa_ref, wb_ref, wc_ref,
                  wo_ref, gp_ref, o_ref):
    merged = None
    for n, (y_ref, w_ref) in enumerate(((ya_ref, wa_ref), (yb_ref, wb_ref), (yc_ref, wc_ref))):
        cols = slice(n * D_MODEL, (n + 1) * D_MODEL)
        term = pg_ref[:, cols].astype(F32) * jnp.dot(y_ref[...], w_ref[...],
                                                     preferred_element_type=F32)
        merged = term if merged is None else merged + term
    out = jnp.dot(merged.astype(BF16), wo_ref[...], preferred_element_type=F32)
    ms = jnp.mean(out * out, axis=-1, keepdims=True)
    o_ref[...] = x_ref[...] + out * lax.rsqrt(ms + EPS) * gp_ref[...]


def _merge(ya, yb, yc, pg, x, wa, wb, wc, w_out, layer, g_post):
    m = x.shape[0]
    tm = ROWS_MERGE
    row = lambda width: pl.BlockSpec((tm, width), lambda i: (i, 0))
    full = lambda a: pl.BlockSpec(a.shape, lambda i: (0, 0))
    of_layer = lambda a: pl.BlockSpec((None,) + a.shape[1:], lambda i: (layer, 0, 0))
    return pl.pallas_call(
        _merge_kernel,
        grid=(m // tm,),
        in_specs=[row(GM_WIDTH), row(SW_WIDTH), row(SB_WIDTH), row(PG_WIDTH), row(D_MODEL),
                  of_layer(wa), of_layer(wb), of_layer(wc), of_layer(w_out), full(g_post)],
        out_specs=row(D_MODEL),
        out_shape=jax.ShapeDtypeStruct((m, D_MODEL), F32),
        compiler_params=pltpu.CompilerParams(
            dimension_semantics=("arbitrary",), vmem_limit_bytes=VMEM_LIMIT),
        name="merge",
    )(ya, yb, yc, pg, x, wa, wb, wc, w_out, g_post)


def _rope_tables(seq):
    half = HEAD_DIM // 2
    freqs = ROPE_THETA ** (-jnp.arange(half, dtype=F32) / half)
    ang = jnp.arange(seq).astype(F32)[:, None] * freqs[None, :]
    cos = jnp.tile(jnp.cos(ang), (1, LANES // half))
    sin = jnp.tile(jnp.concatenate([-jnp.sin(ang), jnp.sin(ang)], axis=-1), (1, LANES // HEAD_DIM))
    return cos, sin


def kernel(x, w_in, gm_w_s, gm_b_s, gm_norm_gain, sw_sinks, w_branch_a, w_branch_b, w_branch_c,
           b_merge, w_out, g_pre, g_post):
    batch, seq, _ = x.shape
    assert seq % ROWS_SB == 0 and (batch * seq) % ROWS_PROJ == 0
    depth = w_in.shape[0]
    cos_t, sin_t = _rope_tables(seq)
    xf = x.reshape(batch * seq, D_MODEL)
    w_in, w_out = w_in.astype(BF16), w_out.astype(BF16)
    wa, wb, wc = w_branch_a.astype(BF16), w_branch_b.astype(BF16), w_branch_c.astype(BF16)
    for l in range(depth):
        ya, pb, pc, pg = _front(xf, g_pre[l][None, :], w_in, l, gm_w_s[l], gm_b_s[l].T,
                                gm_norm_gain[l][None, :], b_merge[l].reshape(1, PG_WIDTH),
                                cos_t, sin_t, seq)
        yb = _swa(pb, sw_sinks[l], batch, seq)
        yc = _sb(pc, batch, seq)
        xf = _merge(ya, yb, yc, pg, xf, wa, wb, wc, w_out, l, g_post[l][None, :])
    return xf.reshape(batch, seq, D_MODEL)
```
